```python
import jax
import jax.numpy as jnp
from jax import lax
import numpy as np

D_MODEL = 1024
BATCH = 4
SEQ = 8192
DEPTH = 2
DEC_BATCH = 32
DEC_SEQ = 64
PAST_LEN = 2048

CHUNK = 64
WINDOW = 128
N_HEADS = 8
N_KV_HEADS = 2
HEAD_DIM = 64
GROUP = N_HEADS // N_KV_HEADS
ATTN_WIDTH = N_HEADS * HEAD_DIM
KV_WIDTH = N_KV_HEADS * HEAD_DIM
CONV_CH = 512
CONV_K = 31
D_MIX = ATTN_WIDTH + CONV_CH
D_IN = ATTN_WIDTH + 2 * KV_WIDTH + 2 * CONV_CH
SPLITS = (ATTN_WIDTH, ATTN_WIDTH + KV_WIDTH, ATTN_WIDTH + 2 * KV_WIDTH, ATTN_WIDTH + 2 * KV_WIDTH + CONV_CH)
N_BAND_CHUNKS = WINDOW // CHUNK
BAND = WINDOW + CHUNK
D_FF = 2816
N_EXPERTS = 8
TOP_K = 2
D_PLE = 256
N_DENSE = (DEPTH + 1) // 2
N_MOE = DEPTH // 2
EPS = 1e-6

kernel_name = 'hybrid_streaming_encoder_step'


def rms_norm(x, g):
    xf = x.astype(jnp.float32)
    y = xf * lax.rsqrt(jnp.mean(xf * xf, axis=-1, keepdims=True) + EPS)
    return (y * g.astype(jnp.float32)).astype(x.dtype)


def layer_norm(x, g, b):
    xf = x.astype(jnp.float32)
    xc = xf - jnp.mean(xf, axis=-1, keepdims=True)
    y = xc * lax.rsqrt(jnp.mean(xc * xc, axis=-1, keepdims=True) + EPS)
    return (y * g.astype(jnp.float32) + b.astype(jnp.float32)).astype(x.dtype)


def alibi_slopes():
    h = jnp.arange(1, N_HEADS + 1, dtype=jnp.float32)
    return jnp.exp2(-8.0 * h / N_HEADS)


def swiglu(x, wg, wu, wd):
    return (jax.nn.silu(x @ wg) * (x @ wu)) @ wd


def moe_swiglu(x, router_w, router_b, wg, wu, wd):
    logits = (x @ router_w).astype(jnp.float32) + router_b.astype(jnp.float32)
    top_v, top_i = lax.top_k(logits, TOP_K)
    gates = jax.nn.softmax(top_v, axis=-1)
    dense_gate = jnp.sum(jax.nn.one_hot(top_i, N_EXPERTS, dtype=jnp.float32) * gates[..., None], axis=-2)
    y = jnp.zeros_like(x)
    for e in range(N_EXPERTS):
        y = y + dense_gate[..., e:e + 1].astype(x.dtype) * swiglu(x, wg[e], wu[e], wd[e])
    return y


def _bands(t, nc):
    b = t.shape[0]
    tp = jnp.pad(t, ((0, 0), (WINDOW, 0), (0, 0), (0, 0)))
    tp = tp.reshape((b, nc + N_BAND_CHUNKS, CHUNK) + t.shape[2:])
    return jnp.concatenate([tp[:, j:j + nc] for j in range(N_BAND_CHUNKS + 1)], axis=2)


def attend(q, k, v, qpos, kpos, valid, sinks):
    s = jnp.einsum('bnqkgd,bnskd->bnkgqs', q, k, preferred_element_type=jnp.float32) * (HEAD_DIM ** -0.5)
    dist = jnp.abs(qpos[:, :, None] - kpos[:, None, :]).astype(jnp.float32)
    slopes = alibi_slopes().reshape(1, N_KV_HEADS, GROUP, 1, 1)
    s = s - slopes * dist[:, None, None]
    s = jnp.where(valid[:, None, None, None, :], s, -jnp.inf)
    sink = sinks.astype(jnp.float32).reshape(N_KV_HEADS, GROUP, 1, 1)
    m = jnp.maximum(jnp.max(s, axis=-1, keepdims=True), sink)
    e = jnp.exp(s - m)
    prob = e / (jnp.sum(e, axis=-1, keepdims=True) + jnp.exp(sink - m))
    return jnp.einsum('bnkgqs,bnskd->bnqkgd', prob.astype(v.dtype), v)


def causal_depthwise_conv(u, prefix, w, b):
    xp = jnp.concatenate([prefix.astype(u.dtype), u], axis=1)
    y = lax.conv_general_dilated(xp, w[:, None, :].astype(u.dtype), (1,), 'VALID',
                                 dimension_numbers=('NWC', 'WIO', 'NWC'), feature_group_count=CONV_CH)
    return y + b, xp[:, -(CONV_K - 1):]


def token_mixers(xn, w_in, q_gain, k_gain, sinks, conv_w, conv_b, cln_g, cln_b, on_attn, on_conv, w_out,
                 k_past, v_past, conv_past):
    bsz, t, _ = xn.shape
    z = xn @ w_in
    q, k, v, a, gl = jnp.split(z, SPLITS, axis=-1)
    q = rms_norm(q.reshape(bsz, t, N_KV_HEADS, GROUP, HEAD_DIM), q_gain)
    k = rms_norm(k.reshape(bsz, t, N_KV_HEADS, HEAD_DIM), k_gain)
    v = v.reshape(bsz, t, N_KV_HEADS, HEAD_DIM)
    if k_past is None:
        nc = t // CHUNK
        qb = q.reshape(bsz, nc, CHUNK, N_KV_HEADS, GROUP, HEAD_DIM)
        kb, vb = _bands(k, nc), _bands(v, nc)
        qpos = jnp.arange(t).reshape(nc, CHUNK)
        kpos = (jnp.arange(nc) * CHUNK)[:, None] - WINDOW + jnp.arange(BAND)[None, :]
        valid = kpos >= 0
        new_k, new_v = k[:, t - WINDOW:], v[:, t - WINDOW:]
        conv_prefix = jnp.zeros((bsz, CONV_K - 1, CONV_CH), xn.dtype)
    else:
        win = k_past.shape[1]
        kc = jnp.concatenate([k_past.astype(k.dtype), k], axis=1)
        vc = jnp.concatenate([v_past.astype(v.dtype), v], axis=1)
        qb, kb, vb = q[:, None], kc[:, None], vc[:, None]
        qpos = (PAST_LEN + jnp.arange(t))[None]
        kpos = (PAST_LEN - win + jnp.arange(win + t))[None]
        valid = jnp.ones(kpos.shape, dtype=bool)
        new_k, new_v = kc[:, -win:], vc[:, -win:]
        conv_prefix = conv_past
    o_attn = attend(qb, kb, vb, qpos, kpos, valid, sinks).reshape(bsz, t, ATTN_WIDTH)
    u = a * jax.nn.sigmoid(gl)
    c, new_conv = causal_depthwise_conv(u, conv_prefix, conv_w, conv_b)
    o_conv = jax.nn.silu(layer_norm(c, cln_g, cln_b))
    mixed = jnp.concatenate([rms_norm(o_attn, on_attn), rms_norm(o_conv, on_conv)], axis=-1) @ w_out
    return mixed, new_k, new_v, new_conv


def run_trunk(x, p, cache_k, cache_v, state_conv, prm):
    h = x
    ks, vs, cs = [], [], []
    for l in range(DEPTH):
        if cache_k is None:
            past = (None, None, None)
        else:
            past = (cache_k[l], cache_v[l], state_conv[l])
        xn = rms_norm(h, prm['norm_mix'][l])
        mixed, nk, nv, ncv = token_mixers(
            xn, prm['w_in'][l], prm['q_gain'][l], prm['k_gain'][l], prm['attn_sinks'][l],
            prm['conv_w'][l], prm['conv_b'][l], prm['conv_ln_g'][l], prm['conv_ln_b'][l],
            prm['out_norm_attn'][l], prm['out_norm_conv'][l], prm['w_out'][l], *past)
        h = h + mixed
        hn = rms_norm(h, prm['norm_ffn'][l])
        i = l // 2
        if l % 2 == 0:
            h = h + swiglu(hn, prm['ffn_gate'][i], prm['ffn_up'][i], prm['ffn_down'][i])
        else:
            h = h + moe_swiglu(hn, prm['router_w'][i], prm['router_b'][i],
                               prm['moe_gate'][i], prm['moe_up'][i], prm['moe_down'][i])
        gate = jax.nn.sigmoid(rms_norm(h, prm['ple_norm'][l]) @ prm['w_ple_gate'][l])
        h = h + gate * (p[l] @ prm['w_ple'][l])
        ks.append(nk)
        vs.append(nv)
        cs.append(ncv)
    return h, jnp.stack(ks), jnp.stack(vs), jnp.stack(cs)


def setup_inputs(seed: int = 0) -> dict:
    key = jax.random.key(seed)
    keys = iter(jax.random.split(key, 40))

    def nrm(shape, scale):
        return jax.random.normal(next(keys), shape, jnp.float32) * scale

    def gain(shape):
        return 1.0 + nrm(shape, 0.02)

    win = min(WINDOW, PAST_LEN)
    return {
        'x_prompt': nrm((BATCH, SEQ, D_MODEL), 1.0),
        'x_sample': nrm((DEC_BATCH, DEC_SEQ, D_MODEL), 1.0),
        'p_prompt': nrm((DEPTH, BATCH, SEQ, D_PLE), 1.0),
        'p_sample': nrm((DEPTH, DEC_BATCH, DEC_SEQ, D_PLE), 1.0),
        'cache_k': nrm((DEPTH, DEC_BATCH, win, N_KV_HEADS, HEAD_DIM), 1.0),
        'cache_v': nrm((DEPTH, DEC_BATCH, win, N_KV_HEADS, HEAD_DIM), 1.0),
        'state_conv': nrm((DEPTH, DEC_BATCH, CONV_K - 1, CONV_CH), 0.5),
        'norm_mix': gain((DEPTH, D_MODEL)),
        'w_in': nrm((DEPTH, D_MODEL, D_IN), D_MODEL ** -0.5),
        'q_gain': gain((DEPTH, HEAD_DIM)),
        'k_gain': gain((DEPTH, HEAD_DIM)),
        'attn_sinks': nrm((DEPTH, N_HEADS), 0.5),
        'conv_w': nrm((DEPTH, CONV_K, CONV_CH), CONV_K ** -0.5),
        'conv_b': nrm((DEPTH, CONV_CH), 0.02),
        'conv_ln_g': gain((DEPTH, CONV_CH)),
        'conv_ln_b': nrm((DEPTH, CONV_CH), 0.02),
        'out_norm_attn': gain((DEPTH, ATTN_WIDTH)),
        'out_norm_conv': gain((DEPTH, CONV_CH)),
        'w_out': nrm((DEPTH, D_MIX, D_MODEL), D_MIX ** -0.5),
        'norm_ffn': gain((DEPTH, D_MODEL)),
        'ffn_gate': nrm((N_DENSE, D_MODEL, D_FF), D_MODEL ** -0.5),
        'ffn_up': nrm((N_DENSE, D_MODEL, D_FF), D_MODEL ** -0.5),
        'ffn_down': nrm((N_DENSE, D_FF, D_MODEL), D_FF ** -0.5),
        'router_w': nrm((N_MOE, D_MODEL, N_EXPERTS), D_MODEL ** -0.5),
        'router_b': nrm((N_MOE, N_EXPERTS), 0.01),
        'moe_gate': nrm((N_MOE, N_EXPERTS, D_MODEL, D_FF), D_MODEL ** -0.5),
        'moe_up': nrm((N_MOE, N_EXPERTS, D_MODEL, D_FF), D_MODEL ** -0.5),
        'moe_down': nrm((N_MOE, N_EXPERTS, D_FF, D_MODEL), D_FF ** -0.5),
        'ple_norm': gain((DEPTH, D_MODEL)),
        'w_ple_gate': nrm((DEPTH, D_MODEL, D_MODEL), D_MODEL ** -0.5),
        'w_ple': nrm((DEPTH, D_PLE, D_MODEL), D_PLE ** -0.5),
    }


def reference(x_prompt, x_sample, p_prompt, p_sample, cache_k, cache_v, state_conv,
              norm_mix, w_in, q_gain, k_gain, attn_sinks, conv_w, conv_b, conv_ln_g, conv_ln_b,
              out_norm_attn, out_norm_conv, w_out, norm_ffn, ffn_gate, ffn_up, ffn_down,
              router_w, router_b, moe_gate, moe_up, moe_down, ple_norm, w_ple_gate, w_ple):
    prm = {
        'norm_mix': norm_mix, 'w_in': w_in, 'q_gain': q_gain, 'k_gain': k_gain,
        'attn_sinks': attn_sinks, 'conv_w': conv_w, 'conv_b': conv_b,
        'conv_ln_g': conv_ln_g, 'conv_ln_b': conv_ln_b,
        'out_norm_attn': out_norm_attn, 'out_norm_conv': out_norm_conv, 'w_out': w_out,
        'norm_ffn': norm_ffn, 'ffn_gate': ffn_gate, 'ffn_up': ffn_up, 'ffn_down': ffn_down,
        'router_w': router_w, 'router_b': router_b,
        'moe_gate': moe_gate, 'moe_up': moe_up, 'moe_down': moe_down,
        'ple_norm': ple_norm, 'w_ple_gate': w_ple_gate, 'w_ple': w_ple,
    }
    y_prompt, win_k_prompt, win_v_prompt, conv_prompt = run_trunk(x_prompt, p_prompt, None, None, None, prm)
    y_sample, win_k_sample, win_v_sample, conv_sample = run_trunk(x_sample, p_sample, cache_k, cache_v, state_conv, prm)
    return (y_prompt, y_sample, win_k_prompt, win_v_prompt, conv_prompt, win_k_sample, win_v_sample, conv_sample)
```

```python
import functools

import jax
import jax.numpy as jnp
from jax import lax
from jax.experimental import pallas as pl
from jax.experimental.pallas import tpu as pltpu

D_MODEL = 1024
CHUNK = 64
WINDOW = 128
N_HEADS = 8
N_KV_HEADS = 2
HEAD_DIM = 64
GROUP = N_HEADS // N_KV_HEADS
ATTN_WIDTH = N_HEADS * HEAD_DIM
KV_WIDTH = N_KV_HEADS * HEAD_DIM
QK_WIDTH = ATTN_WIDTH + KV_WIDTH
CONV_CH = 512
CONV_K = 31
CONV_PAD = 32
D_IN = ATTN_WIDTH + 2 * KV_WIDTH + 2 * CONV_CH
BAND = WINDOW + CHUNK
D_FF = 2816
N_EXPERTS = 8
D_PLE = 256
EPS = 1e-6
LANES = 128

F32 = jnp.float32
BF16 = jnp.bfloat16

VMEM_LIMIT = 56 * 1024 * 1024


def _rms(x, g):
    return x * lax.rsqrt(jnp.mean(x * x, axis=-1, keepdims=True) + EPS) * g


def _in_proj_kernel(h_ref, nrm_ref, w_ref, gain_ref, hsum_ref, q_ref, kv_ref, u_ref):
    xn = _rms(h_ref[...], nrm_ref[...])
    z = jnp.dot(xn.astype(BF16), w_ref[...], preferred_element_type=F32)
    qk = z[:, :QK_WIDTH]
    ss = jnp.dot((qk * qk).astype(BF16), hsum_ref[...], preferred_element_type=F32)
    qkn = qk * lax.rsqrt(ss * (1.0 / HEAD_DIM) + EPS) * gain_ref[...]
    q_ref[...] = qkn[:, :ATTN_WIDTH].astype(q_ref.dtype)
    kv_ref[:, :KV_WIDTH] = qkn[:, ATTN_WIDTH:]
    kv_ref[:, KV_WIDTH:] = z[:, QK_WIDTH:QK_WIDTH + KV_WIDTH]
    a = z[:, QK_WIDTH + KV_WIDTH:QK_WIDTH + KV_WIDTH + CONV_CH]
    gl = z[:, QK_WIDTH + KV_WIDTH + CONV_CH:]
    u_ref[...] = a * jax.nn.sigmoid(gl)


def _in_proj(h, nrm, w_in, qk_gain, hsum, tm):
    m = h.shape[0]
    row = lambda i: (i, 0)
    fixed = lambda i: (0, 0)
    return pl.pallas_call(
        _in_proj_kernel,
        grid=(m // tm,),
        in_specs=[
            pl.BlockSpec((tm, D_MODEL), row),
            pl.BlockSpec((1, D_MODEL), fixed),
            pl.BlockSpec((D_MODEL, D_IN), fixed),
            pl.BlockSpec((1, QK_WIDTH), fixed),
            pl.BlockSpec((QK_WIDTH, QK_WIDTH), fixed),
        ],
        out_specs=[
            pl.BlockSpec((tm, ATTN_WIDTH), row),
            pl.BlockSpec((tm, 2 * KV_WIDTH), row),
            pl.BlockSpec((tm, CONV_CH), row),
        ],
        out_shape=[
            jax.ShapeDtypeStruct((m, ATTN_WIDTH), BF16),
            jax.ShapeDtypeStruct((m, 2 * KV_WIDTH), F32),
            jax.ShapeDtypeStruct((m, CONV_CH), F32),
        ],
        compiler_params=pltpu.CompilerParams(
            dimension_semantics=("arbitrary",), vmem_limit_bytes=VMEM_LIMIT),
        name="in_proj",
    )(h, nrm, w_in, qk_gain, hsum)


def _mixer_kernel(q_ref, kv_ref, kvp_ref, u_ref, up_ref, h_ref, bias_ref, sink_ref,
                  cw_ref, cb_ref, lng_ref, lnb_ref, ona_ref, onc_ref, wout_ref, nffn_ref,
                  *rest, tq, has_past, with_router):
    if with_router:
        rw_ref, rb_ref, h1_ref, hn_ref, gate_ref, kvx_ref, ux_ref, oa_ref = rest
    else:
        h1_ref, hn_ref, kvx_ref, ux_ref, oa_ref = rest
    t = pl.program_id(1)
    nch = tq // CHUNK

    kvx_ref[:WINDOW, :] = kvp_ref[...].astype(BF16)
    kvx_ref[WINDOW:, :] = kv_ref[...].astype(BF16)

    for c in range(nch):
        qc = q_ref[c * CHUNK:(c + 1) * CHUNK, :]
        kvb = kvx_ref[c * CHUNK:c * CHUNK + BAND, :]
        if not has_past and c < WINDOW // CHUNK:
            kpos = lax.broadcasted_iota(jnp.int32, (1, BAND), 1) + (t * tq + c * CHUNK - WINDOW)
            valid = kpos >= 0
        else:
            valid = None
        outs = []
        for j in range(N_KV_HEADS):
            qs = jnp.concatenate(
                [qc[:, (j * GROUP + g) * HEAD_DIM:(j * GROUP + g + 1) * HEAD_DIM] for g in range(GROUP)],
                axis=0)
            kj = kvb[:, j * HEAD_DIM:(j + 1) * HEAD_DIM]
            vj = kvb[:, KV_WIDTH + j * HEAD_DIM:KV_WIDTH + (j + 1) * HEAD_DIM]
            s = lax.dot_general(qs, kj, (((1,), (1,)), ((), ())), preferred_element_type=F32)
            s = s - bias_ref[j]
            if valid is not None:
                s = jnp.where(valid, s, -jnp.inf)
            sink = sink_ref[j]
            mx = jnp.maximum(jnp.max(s, axis=-1, keepdims=True), sink)
            e = jnp.exp(s - mx)
            prob = e / (jnp.sum(e, axis=-1, keepdims=True) + jnp.exp(sink - mx))
            o = jnp.dot(prob.astype(BF16), vj, preferred_element_type=F32)
            outs.extend(o[g * CHUNK:(g + 1) * CHUNK, :] for g in range(GROUP))
        oa_ref[c * CHUNK:(c + 1) * CHUNK, :] = jnp.concatenate(outs, axis=1)

    if has_past:
        ux_ref[:CONV_PAD, :] = up_ref[...]
    else:
        ux_ref[:CONV_PAD, :] = jnp.where(t > 0, up_ref[...], 0.0)
    ux_ref[CONV_PAD:, :] = u_ref[...]
    off = CONV_PAD - (CONV_K - 1)
    acc = ux_ref[off:off + tq, :] * cw_ref[0:1, :]
    for k in range(1, CONV_K):
        acc = acc + ux_ref[off + k:off + k + tq, :] * cw_ref[k:k + 1, :]
    cv = acc + cb_ref[...]
    mu = jnp.mean(cv, axis=-1, keepdims=True)
    xc = cv - mu
    ln = xc * lax.rsqrt(jnp.mean(xc * xc, axis=-1, keepdims=True) + EPS) * lng_ref[...] + lnb_ref[...]
    oc = ln * jax.nn.sigmoid(ln)

    an = _rms(oa_ref[...], ona_ref[...]).astype(BF16)
    cn = _rms(oc, onc_ref[...]).astype(BF16)
    mixed = (jnp.dot(an, wout_ref[:ATTN_WIDTH, :], preferred_element_type=F32)
             + jnp.dot(cn, wout_ref[ATTN_WIDTH:, :], preferred_element_type=F32))
    h1 = h_ref[...] + mixed
    h1_ref[...] = h1
    hn = _rms(h1, nffn_ref[...])
    hn_ref[...] = hn.astype(BF16)

    if with_router:
        logits = jnp.dot(hn, rw_ref[...], preferred_element_type=F32,
                         precision=lax.Precision.HIGHEST) + rb_ref[...]
        col = lax.broadcasted_iota(jnp.int32, logits.shape, 1)
        logits = jnp.where(col < N_EXPERTS, logits, -jnp.inf)
        m1 = jnp.max(logits, axis=-1, keepdims=True)
        i1 = jnp.min(jnp.where(logits == m1, col, LANES), axis=-1, keepdims=True)
        rest_l = jnp.where(col == i1, -jnp.inf, logits)
        m2 = jnp.max(rest_l, axis=-1, keepdims=True)
        i2 = jnp.min(jnp.where(rest_l == m2, col, LANES), axis=-1, keepdims=True)
        e2 = jnp.exp(m2 - m1)
        den = 1.0 + e2
        gate_ref[...] = jnp.where(col == i1, 1.0 / den, 0.0) + jnp.where(col == i2, e2 / den, 0.0)


def _mixer(q, kv, kv_prev, u, u_prev, h, prm, *, nseq, tq, has_past):
    m = h.shape[0]
    seq = m // nseq
    nt = seq // tq
    with_router = prm["router_w"] is not None
    cur = lambda b, t: (b * nt + t, 0)
    fixed2 = lambda b, t: (0, 0)
    fixed3 = lambda b, t: (0, 0, 0)
    if has_past:
        kvp_map = lambda b, t: (b, 0)
        up_map = lambda b, t: (b, 0)
    else:
        kvp_map = lambda b, t: (jnp.maximum(b * (seq // WINDOW) + t * (tq // WINDOW) - 1, 0), 0)
        up_map = lambda b, t: (jnp.maximum(b * (seq // CONV_PAD) + t * (tq // CONV_PAD) - 1, 0), 0)
    in_specs = [
        pl.BlockSpec((tq, ATTN_WIDTH), cur),
        pl.BlockSpec((tq, 2 * KV_WIDTH), cur),
        pl.BlockSpec((WINDOW, 2 * KV_WIDTH), kvp_map),
        pl.BlockSpec((tq, CONV_CH), cur),
        pl.BlockSpec((CONV_PAD, CONV_CH), up_map),
        pl.BlockSpec((tq, D_MODEL), cur),
        pl.BlockSpec((N_KV_HEADS, GROUP * CHUNK, BAND), fixed3),
        pl.BlockSpec((N_KV_HEADS, GROUP * CHUNK, 1), fixed3),
        pl.BlockSpec((CONV_PAD, CONV_CH), fixed2),
        pl.BlockSpec((1, CONV_CH), fixed2),
        pl.BlockSpec((1, CONV_CH), fixed2),
        pl.BlockSpec((1, CONV_CH), fixed2),
        pl.BlockSpec((1, ATTN_WIDTH), fixed2),
        pl.BlockSpec((1, CONV_CH), fixed2),
        pl.BlockSpec((D_MODEL, D_MODEL), fixed2),
        pl.BlockSpec((1, D_MODEL), fixed2),
    ]
    args = [q, kv, kv_prev, u, u_prev, h, prm["bias"], prm["sinks"], prm["conv_w"], prm["conv_b"],
            prm["cln_g"], prm["cln_b"], prm["on_attn"], prm["on_conv"], prm["w_out"], prm["norm_ffn"]]
    out_specs = [pl.BlockSpec((tq, D_MODEL), cur), pl.BlockSpec((tq, D_MODEL), cur)]
    out_shape = [jax.ShapeDtypeStruct((m, D_MODEL), F32), jax.ShapeDtypeStruct((m, D_MODEL), BF16)]
    if with_router:
        in_specs += [pl.BlockSpec((D_MODEL, LANES), fixed2), pl.BlockSpec((1, LANES), fixed2)]
        args += [prm["router_w"], prm["router_b"]]
        out_specs.append(pl.BlockSpec((tq, LANES), cur))
        out_shape.append(jax.ShapeDtypeStruct((m, LANES), F32))
    return pl.pallas_call(
        functools.partial(_mixer_kernel, tq=tq, has_past=has_past, with_router=with_router),
        grid=(nseq, nt),
        in_specs=in_specs,
        out_specs=out_specs,
        out_shape=out_shape,
        scratch_shapes=[
            pltpu.VMEM((WINDOW + tq, 2 * KV_WIDTH), BF16),
            pltpu.VMEM((CONV_PAD + tq, CONV_CH), F32),
            pltpu.VMEM((tq, ATTN_WIDTH), F32),
        ],
        compiler_params=pltpu.CompilerParams(
            dimension_semantics=("arbitrary", "arbitrary"), vmem_limit_bytes=VMEM_LIMIT),
        name="mixer_past" if has_past else "mixer",
    )(*args)


def _ffn_kernel(x_ref, h_ref, p_ref, wg_ref, wu_ref, wd_ref, npl_ref, wpg_ref, wpl_ref, *rest, gated):
    if gated:
        gate_ref, o_ref = rest
    else:
        (o_ref,) = rest
    e = pl.program_id(1)
    f = pl.program_id(2)

    @pl.when((e == 0) & (f == 0))
    def _():
        o_ref[...] = h_ref[...]

    x = x_ref[...]
    g = jnp.dot(x, wg_ref[0], preferred_element_type=F32)
    u = jnp.dot(x, wu_ref[0], preferred_element_type=F32)
    a = (g * jax.nn.sigmoid(g) * u).astype(BF16)
    y = jnp.dot(a, wd_ref[0], preferred_element_type=F32)
    if gated:
        col = lax.broadcasted_iota(jnp.int32, gate_ref.shape, 1)
        y = jnp.sum(jnp.where(col == e, gate_ref[...], 0.0), axis=-1, keepdims=True) * y
    o_ref[...] += y

    @pl.when((e == pl.num_programs(1) - 1) & (f == pl.num_programs(2) - 1))
    def _():
        h2 = o_ref[...]
        r = _rms(h2, npl_ref[...]).astype(BF16)
        gate = jax.nn.sigmoid(jnp.dot(r, wpg_ref[...], preferred_element_type=F32))
        pe = jnp.dot(p_ref[...].astype(BF16), wpl_ref[...], preferred_element_type=F32)
        o_ref[...] = h2 + gate * pe


def _ffn(x, h, p, wg, wu, wd, npl, wpg, wpl, gates, tm, tf):
    m = x.shape[0]
    ne = wg.shape[0]
    gated = gates is not None
    row = lambda i, e, f: (i, 0)
    fixed = lambda i, e, f: (0, 0)
    in_specs = [
        pl.BlockSpec((tm, D_MODEL), row),
        pl.BlockSpec((tm, D_MODEL), row),
        pl.BlockSpec((tm, D_PLE), row),
        pl.BlockSpec((1, D_MODEL, tf), lambda i, e, f: (e, 0, f)),
        pl.BlockSpec((1, D_MODEL, tf), lambda i, e, f: (e, 0, f)),
        pl.BlockSpec((1, tf, D_MODEL), lambda i, e, f: (e, f, 0)),
        pl.BlockSpec((1, D_MODEL), fixed),
        pl.BlockSpec((D_MODEL, D_MODEL), fixed),
        pl.BlockSpec((D_PLE, D_MODEL), fixed),
    ]
    args = [x, h, p, wg, wu, wd, npl, wpg, wpl]
    if gated:
        in_specs.append(pl.BlockSpec((tm, LANES), row))
        args.append(gates)
    return pl.pallas_call(
        functools.partial(_ffn_kernel, gated=gated),
        grid=(m // tm, ne, D_FF // tf),
        in_specs=in_specs,
        out_specs=pl.BlockSpec((tm, D_MODEL), row),
        out_shape=jax.ShapeDtypeStruct((m, D_MODEL), F32),
        compiler_params=pltpu.CompilerParams(
            dimension_semantics=("arbitrary", "arbitrary", "arbitrary"), vmem_limit_bytes=VMEM_LIMIT),
        name="ffn_gated" if gated else "ffn",
    )(*args)


def _alibi_bias():
    slopes = jnp.exp2(-8.0 * jnp.arange(1, N_HEADS + 1, dtype=F32) / N_HEADS)
    qi = jnp.arange(CHUNK, dtype=jnp.int32)[:, None]
    kj = jnp.arange(BAND, dtype=jnp.int32)[None, :]
    dist = jnp.abs(qi + WINDOW - kj).astype(F32)
    bias = slopes[:, None, None] * dist[None]
    return bias.reshape(N_KV_HEADS, GROUP * CHUNK, BAND)


def _head_sum_matrix():
    head = jnp.arange(QK_WIDTH, dtype=jnp.int32) // HEAD_DIM
    return (head[:, None] == head[None, :]).astype(BF16)


def _layer_params(l, norm_mix, w_in, q_gain, k_gain, attn_sinks, conv_w, conv_b, conv_ln_g, conv_ln_b,
                  out_norm_attn, out_norm_conv, w_out, norm_ffn, router_w, router_b, ple_norm,
                  w_ple_gate, w_ple):
    scale = HEAD_DIM ** -0.5
    prm = {
        "norm_mix": norm_mix[l][None, :],
        "w_in": w_in[l].astype(BF16),
        "qk_gain": jnp.concatenate([jnp.tile(q_gain[l] * scale, N_HEADS), jnp.tile(k_gain[l], N_KV_HEADS)])[None, :],
        "sinks": jnp.broadcast_to(attn_sinks[l].reshape(N_KV_HEADS, GROUP, 1, 1),
                                  (N_KV_HEADS, GROUP, CHUNK, 1)).reshape(N_KV_HEADS, GROUP * CHUNK, 1),
        "conv_w": jnp.pad(conv_w[l], ((0, CONV_PAD - CONV_K), (0, 0))),
        "conv_b": conv_b[l][None, :],
        "cln_g": conv_ln_g[l][None, :],
        "cln_b": conv_ln_b[l][None, :],
        "on_attn": out_norm_attn[l][None, :],
        "on_conv": out_norm_conv[l][None, :],
        "w_out": w_out[l].astype(BF16),
        "norm_ffn": norm_ffn[l][None, :],
        "ple_norm": ple_norm[l][None, :],
        "w_ple_gate": w_ple_gate[l].astype(BF16),
        "w_ple": w_ple[l].astype(BF16),
        "router_w": None,
        "router_b": None,
    }
    if l % 2 == 1:
        i = l // 2
        prm["router_w"] = jnp.pad(router_w[i], ((0, 0), (0, LANES - N_EXPERTS)))
        prm["router_b"] = jnp.pad(router_b[i], (0, LANES - N_EXPERTS))[None, :]
    return prm


def _trunk(x, p, cache_k, cache_v, state_conv, layers, ffn_w, bias, hsum, *, tm, tq):
    nseq, seq, _ = x.shape
    m = nseq * seq
    has_past = cache_k is not None
    h = x.reshape(m, D_MODEL)
    win_k, win_v, convs = [], [], []
    for l, prm in enumerate(layers):
        q, kv, u = _in_proj(h, prm["norm_mix"], prm["w_in"], prm["qk_gain"], hsum, tm)
        if has_past:
            win = cache_k.shape[2]
            kv_prev = jnp.concatenate([cache_k[l].reshape(nseq * win, KV_WIDTH),
                                       cache_v[l].reshape(nseq * win, KV_WIDTH)], axis=1)
            u_prev = jnp.pad(state_conv[l], ((0, 0), (CONV_PAD - (CONV_K - 1), 0), (0, 0))).reshape(
                nseq * CONV_PAD, CONV_CH)
        else:
            kv_prev, u_prev = kv, u
        outs = _mixer(q, kv, kv_prev, u, u_prev, h, dict(prm, bias=bias), nseq=nseq, tq=tq, has_past=has_past)
        h1, hn = outs[0], outs[1]
        gates = outs[2] if len(outs) == 3 else None
        wg, wu, wd = ffn_w[l]
        h = _ffn(hn, h1, p[l].reshape(m, D_PLE), wg, wu, wd, prm["ple_norm"], prm["w_ple_gate"], prm["w_ple"],
                 gates, tm, D_FF // 2)
        kv3 = kv.reshape(nseq, seq, 2 * KV_WIDTH)
        u3 = u.reshape(nseq, seq, CONV_CH)
        if has_past:
            kv3 = jnp.concatenate([kv_prev.reshape(nseq, win, 2 * KV_WIDTH), kv3], axis=1)[:, -win:]
            u3 = jnp.concatenate([state_conv[l], u3], axis=1)
        else:
            kv3 = kv3[:, seq - WINDOW:]
        win_k.append(kv3[..., :KV_WIDTH].reshape(nseq, -1, N_KV_HEADS, HEAD_DIM))
        win_v.append(kv3[..., KV_WIDTH:].reshape(nseq, -1, N_KV_HEADS, HEAD_DIM))
        convs.append(u3[:, -(CONV_K - 1):])
    return h.reshape(nseq, seq, D_MODEL), jnp.stack(win_k), jnp.stack(win_v), jnp.stack(convs)


def kernel(x_prompt, x_sample, p_prompt, p_sample, cache_k, cache_v, state_conv, norm_mix, w_in, q_gain, k_gain, attn_sinks, conv_w, conv_b, conv_ln_g, conv_ln_b, out_norm_attn, out_norm_conv, w_out, norm_ffn, ffn_gate, ffn_up, ffn_down, router_w, router_b, moe_gate, moe_up, moe_down, ple_norm, w_ple_gate, w_ple):
    depth = w_in.shape[0]
    layers = [
        _layer_params(l, norm_mix, w_in, q_gain, k_gain, attn_sinks, conv_w, conv_b, conv_ln_g, conv_ln_b,
                      out_norm_attn, out_norm_conv, w_out, norm_ffn, router_w, router_b, ple_norm,
                      w_ple_gate, w_ple)
        for l in range(depth)
    ]
    ffn_w = []
    for l in range(depth):
        i = l // 2
        if l % 2 == 0:
            ffn_w.append((ffn_gate[i][None].astype(BF16), ffn_up[i][None].astype(BF16),
                          ffn_down[i][None].astype(BF16)))
        else:
            ffn_w.append((moe_gate[i].astype(BF16), moe_up[i].astype(BF16), moe_down[i].astype(BF16)))
    bias = _alibi_bias()
    hsum = _head_sum_matrix()
    tm_p = min(512, x_prompt.shape[0] * x_prompt.shape[1])
    tq_p = min(512, x_prompt.shape[1])
    y_p, wk_p, wv_p, cv_p = _trunk(x_prompt, p_prompt, None, None, None, layers, ffn_w, bias, hsum,
                                   tm=tm_p, tq=tq_p)
    tm_s = min(512, x_sample.shape[0] * x_sample.shape[1])
    y_s, wk_s, wv_s, cv_s = _trunk(x_sample, p_sample, cache_k, cache_v, state_conv, layers, ffn_w, bias, hsum,
                                   tm=tm_s, tq=x_sample.shape[1])
    return (y_p, y_s, wk_p, wv_p, cv_p, wk_s, wv_s, cv_s)
```

```python
import functools

import jax
import jax.numpy as jnp
from jax import lax
from jax.experimental import pallas as pl
from jax.experimental.pallas import tpu as pltpu

D_MODEL = 1024
CHUNK = 64
WINDOW = 128
N_HEADS = 8
N_KV_HEADS = 2
HEAD_DIM = 64
GROUP = N_HEADS // N_KV_HEADS
ATTN_WIDTH = N_HEADS * HEAD_DIM
KV_WIDTH = N_KV_HEADS * HEAD_DIM
QK_WIDTH = ATTN_WIDTH + KV_WIDTH
CONV_CH = 512
CONV_K = 31
SUBLANES = 8
CONV_PAD = 32
CONV_ROWS = 64
CONV_ACCS = 2
D_IN = ATTN_WIDTH + 2 * KV_WIDTH + 2 * CONV_CH
BAND = WINDOW + CHUNK
D_FF = 2816
N_EXPERTS = 8
TOP_K = 2
N_FILL = 2 * N_EXPERTS
D_PLE = 256
DMA_UNROLL = 8
EPS = 1e-6
LANES = 128

F32 = jnp.float32
BF16 = jnp.bfloat16

VMEM_LIMIT = 56 * 1024 * 1024


def _rms(x, g):
    return x * lax.rsqrt(jnp.mean(x * x, axis=-1, keepdims=True) + EPS) * g


def _in_proj_kernel(h_ref, nrm_ref, w_ref, gain_ref, hsum_ref, q_ref, kv_ref, u_ref):
    xn = _rms(h_ref[...], nrm_ref[...])
    z = jnp.dot(xn.astype(BF16), w_ref[...], preferred_element_type=F32)
    qk = z[:, :QK_WIDTH]
    ss = jnp.dot((qk * qk).astype(BF16), hsum_ref[...], preferred_element_type=F32)
    qkn = qk * lax.rsqrt(ss * (1.0 / HEAD_DIM) + EPS) * gain_ref[...]
    q_ref[...] = qkn[:, :ATTN_WIDTH].astype(q_ref.dtype)
    kv_ref[:, :KV_WIDTH] = qkn[:, ATTN_WIDTH:]
    kv_ref[:, KV_WIDTH:] = z[:, QK_WIDTH:QK_WIDTH + KV_WIDTH]
    a = z[:, QK_WIDTH + KV_WIDTH:QK_WIDTH + KV_WIDTH + CONV_CH]
    gl = z[:, QK_WIDTH + KV_WIDTH + CONV_CH:]
    u_ref[...] = a * jax.nn.sigmoid(gl)


def _in_proj(h, nrm, w_in, qk_gain, hsum, tm):
    m = h.shape[0]
    row = lambda i: (i, 0)
    fixed = lambda i: (0, 0)
    return pl.pallas_call(
        _in_proj_kernel,
        grid=(m // tm,),
        in_specs=[
            pl.BlockSpec((tm, D_MODEL), row),
            pl.BlockSpec((1, D_MODEL), fixed),
            pl.BlockSpec((D_MODEL, D_IN), fixed),
            pl.BlockSpec((1, QK_WIDTH), fixed),
            pl.BlockSpec((QK_WIDTH, QK_WIDTH), fixed),
        ],
        out_specs=[
            pl.BlockSpec((tm, ATTN_WIDTH), row),
            pl.BlockSpec((tm, 2 * KV_WIDTH), row),
            pl.BlockSpec((tm, CONV_CH), row),
        ],
        out_shape=[
            jax.ShapeDtypeStruct((m, ATTN_WIDTH), BF16),
            jax.ShapeDtypeStruct((m, 2 * KV_WIDTH), F32),
            jax.ShapeDtypeStruct((m, CONV_CH), F32),
        ],
        compiler_params=pltpu.CompilerParams(
            dimension_semantics=("arbitrary",), vmem_limit_bytes=VMEM_LIMIT),
        name="in_proj",
    )(h, nrm, w_in, qk_gain, hsum)


def _mixer_kernel(q_ref, kv_ref, kvp_ref, u_ref, up_ref, h_ref, bias_ref, sink_ref,
                  cw_ref, cb_ref, lng_ref, lnb_ref, ona_ref, onc_ref, wout_ref, nffn_ref,
                  *rest, tq, has_past, with_router):
    if with_router:
        rw_ref, rb_ref, h1_ref, hn_ref, route_ref, kvx_ref, ux_ref, oa_ref, cv_ref = rest
    else:
        h1_ref, hn_ref, kvx_ref, ux_ref, oa_ref, cv_ref = rest
    t = pl.program_id(1)
    nch = tq // CHUNK

    kvx_ref[:WINDOW, :] = kvp_ref[...].astype(BF16)
    kvx_ref[WINDOW:, :] = kv_ref[...].astype(BF16)

    for c in range(nch):
        qc = q_ref[c * CHUNK:(c + 1) * CHUNK, :]
        kvb = kvx_ref[c * CHUNK:c * CHUNK + BAND, :]
        if not has_past and c < WINDOW // CHUNK:
            kpos = lax.broadcasted_iota(jnp.int32, (BAND, 1), 0) + (t * tq + c * CHUNK - WINDOW)
            valid = kpos >= 0
        else:
            valid = None
        outs = []
        for j in range(N_KV_HEADS):
            qs = jnp.concatenate(
                [qc[:, (j * GROUP + g) * HEAD_DIM:(j * GROUP + g + 1) * HEAD_DIM] for g in range(GROUP)],
                axis=0)
            kj = kvb[:, j * HEAD_DIM:(j + 1) * HEAD_DIM]
            vj = kvb[:, KV_WIDTH + j * HEAD_DIM:KV_WIDTH + (j + 1) * HEAD_DIM]
            s = lax.dot_general(kj, qs, (((1,), (1,)), ((), ())), preferred_element_type=F32)
            s = s - bias_ref[j]
            if valid is not None:
                s = jnp.where(valid, s, -jnp.inf)
            sink = sink_ref[j]
            mx = jnp.maximum(jnp.max(s, axis=0, keepdims=True), sink)
            e = jnp.exp(s - mx)
            den = jnp.sum(e, axis=0, keepdims=True) + jnp.exp(sink - mx)
            prob = (e * (1.0 / den)).astype(BF16)
            o = lax.dot_general(prob, vj, (((0,), (0,)), ((), ())), preferred_element_type=F32)
            outs.extend(o[g * CHUNK:(g + 1) * CHUNK, :] for g in range(GROUP))
        oa_ref[c * CHUNK:(c + 1) * CHUNK, :] = jnp.concatenate(outs, axis=1)

    if has_past:
        ux_ref[0, :CONV_PAD, :] = up_ref[...]
    else:
        ux_ref[0, :CONV_PAD, :] = jnp.where(t > 0, up_ref[...], 0.0)
    ux_ref[0, CONV_PAD:, :] = u_ref[...]
    shifted_rows = tq + CONV_PAD - SUBLANES
    for r in range(1, SUBLANES):
        ux_ref[r, :shifted_rows, :] = ux_ref[0, r:r + shifted_rows, :]
    lead = CONV_PAD - (CONV_K - 1)

    for cg in range(CONV_CH // LANES):
        lanes = slice(cg * LANES, (cg + 1) * LANES)
        taps = [cw_ref[k:k + 1, lanes] for k in range(CONV_K)]

        def conv_rows(i, carry, lanes=lanes, taps=taps):
            r0 = pl.multiple_of(i * CONV_ROWS, CONV_ROWS)
            accs = [None] * CONV_ACCS
            for r in range(SUBLANES):
                steps = [(k, (lead + k) // SUBLANES) for k in range(CONV_K) if (lead + k) % SUBLANES == r]
                slab = ux_ref[r, pl.ds(r0, CONV_ROWS + steps[-1][1] * SUBLANES), lanes]
                for k, a in steps:
                    term = slab[a * SUBLANES:a * SUBLANES + CONV_ROWS, :] * taps[k]
                    accs[k % CONV_ACCS] = term if accs[k % CONV_ACCS] is None else accs[k % CONV_ACCS] + term
            cv_ref[pl.ds(r0, CONV_ROWS), lanes] = functools.reduce(lambda x, y: x + y, accs) + cb_ref[:, lanes]
            return carry

        lax.fori_loop(0, tq // CONV_ROWS, conv_rows, 0)

    cv = cv_ref[...]
    mu = jnp.mean(cv, axis=-1, keepdims=True)
    xc = cv - mu
    ln = xc * lax.rsqrt(jnp.mean(xc * xc, axis=-1, keepdims=True) + EPS) * lng_ref[...] + lnb_ref[...]
    oc = ln * jax.nn.sigmoid(ln)
    cn = _rms(oc, onc_ref[...]).astype(BF16)

    an = _rms(oa_ref[...], ona_ref[...]).astype(BF16)
    mixed = (jnp.dot(an, wout_ref[:ATTN_WIDTH, :], preferred_element_type=F32)
             + jnp.dot(cn, wout_ref[ATTN_WIDTH:, :], preferred_element_type=F32))
    h1 = h_ref[...] + mixed
    h1_ref[...] = h1
    hn = _rms(h1, nffn_ref[...])
    hn_ref[...] = hn.astype(hn_ref.dtype)

    if with_router:
        hi = hn.astype(BF16)
        lo = (hn - hi.astype(F32)).astype(BF16)
        part = jnp.dot(hi, rw_ref[...], preferred_element_type=F32)
        logits = (part[:, :LANES] + part[:, LANES:]
                  + jnp.dot(lo, rw_ref[:, :LANES], preferred_element_type=F32) + rb_ref[...])
        col = lax.broadcasted_iota(jnp.int32, logits.shape, 1)
        logits = jnp.where(col < N_EXPERTS, logits, -jnp.inf)
        m1 = jnp.max(logits, axis=-1, keepdims=True)
        i1 = jnp.min(jnp.where(logits == m1, col, LANES), axis=-1, keepdims=True)
        rest_l = jnp.where(col == i1, -jnp.inf, logits)
        m2 = jnp.max(rest_l, axis=-1, keepdims=True)
        i2 = jnp.min(jnp.where(rest_l == m2, col, LANES), axis=-1, keepdims=True)
        e2 = jnp.exp(m2 - m1)
        den = 1.0 + e2
        route_ref[...] = jnp.where(
            col == 0, i1.astype(F32),
            jnp.where(col == 1, i2.astype(F32),
                      jnp.where(col == 2, 1.0 / den, jnp.where(col == 3, e2 / den, 0.0))))


def _mixer(q, kv, kv_prev, u, u_prev, h, prm, *, nseq, tq, has_past):
    m = h.shape[0]
    seq = m // nseq
    nt = seq // tq
    with_router = prm["router_w"] is not None
    cur = lambda b, t: (b * nt + t, 0)
    fixed2 = lambda b, t: (0, 0)
    fixed3 = lambda b, t: (0, 0, 0)
    if has_past:
        kvp_map = lambda b, t: (b, 0)
        up_map = lambda b, t: (b, 0)
    else:
        kvp_map = lambda b, t: (jnp.maximum(b * (seq // WINDOW) + t * (tq // WINDOW) - 1, 0), 0)
        up_map = lambda b, t: (jnp.maximum(b * (seq // CONV_PAD) + t * (tq // CONV_PAD) - 1, 0), 0)
    in_specs = [
        pl.BlockSpec((tq, ATTN_WIDTH), cur),
        pl.BlockSpec((tq, 2 * KV_WIDTH), cur),
        pl.BlockSpec((WINDOW, 2 * KV_WIDTH), kvp_map),
        pl.BlockSpec((tq, CONV_CH), cur),
        pl.BlockSpec((CONV_PAD, CONV_CH), up_map),
        pl.BlockSpec((tq, D_MODEL), cur),
        pl.BlockSpec((N_KV_HEADS, BAND, GROUP * CHUNK), fixed3),
        pl.BlockSpec((N_KV_HEADS, 1, GROUP * CHUNK), fixed3),
        pl.BlockSpec((CONV_PAD, CONV_CH), fixed2),
        pl.BlockSpec((1, CONV_CH), fixed2),
        pl.BlockSpec((1, CONV_CH), fixed2),
        pl.BlockSpec((1, CONV_CH), fixed2),
        pl.BlockSpec((1, ATTN_WIDTH), fixed2),
        pl.BlockSpec((1, CONV_CH), fixed2),
        pl.BlockSpec((D_MODEL, D_MODEL), fixed2),
        pl.BlockSpec((1, D_MODEL), fixed2),
    ]
    args = [q, kv, kv_prev, u, u_prev, h, prm["bias"], prm["sinks"], prm["conv_w"], prm["conv_b"],
            prm["cln_g"], prm["cln_b"], prm["on_attn"], prm["on_conv"], prm["w_out"], prm["norm_ffn"]]
    out_specs = [pl.BlockSpec((tq, D_MODEL), cur), pl.BlockSpec((tq, D_MODEL), cur)]
    out_shape = [jax.ShapeDtypeStruct((m, D_MODEL), F32),
                 jax.ShapeDtypeStruct((m, D_MODEL), F32 if with_router else BF16)]
    if with_router:
        in_specs += [pl.BlockSpec((D_MODEL, 2 * LANES), fixed2), pl.BlockSpec((1, LANES), fixed2)]
        args += [prm["router_w"], prm["router_b"]]
        out_specs.append(pl.BlockSpec((tq, LANES), cur))
        out_shape.append(jax.ShapeDtypeStruct((m, LANES), F32))
    return pl.pallas_call(
        functools.partial(_mixer_kernel, tq=tq, has_past=has_past, with_router=with_router),
        grid=(nseq, nt),
        in_specs=in_specs,
        out_specs=out_specs,
        out_shape=out_shape,
        scratch_shapes=[
            pltpu.VMEM((WINDOW + tq, 2 * KV_WIDTH), BF16),
            pltpu.VMEM((SUBLANES, CONV_PAD + tq, CONV_CH), F32),
            pltpu.VMEM((tq, ATTN_WIDTH), F32),
            pltpu.VMEM((tq, CONV_CH), F32),
        ],
        compiler_params=pltpu.CompilerParams(
            dimension_semantics=("arbitrary", "arbitrary"), vmem_limit_bytes=VMEM_LIMIT),
        name="mixer_past" if has_past else "mixer",
    )(*args)


def _ple(h2, p_ref, npl_ref, wpg_ref, wpl_ref):
    r = _rms(h2, npl_ref[...]).astype(BF16)
    gate = jax.nn.sigmoid(jnp.dot(r, wpg_ref[...], preferred_element_type=F32))
    pe = jnp.dot(p_ref[...].astype(BF16), wpl_ref[...], preferred_element_type=F32)
    return h2 + gate * pe


def _swiglu(x, wg, wu, wd):
    g = jnp.dot(x, wg, preferred_element_type=F32)
    u = jnp.dot(x, wu, preferred_element_type=F32)
    a = (g * jax.nn.sigmoid(g) * u).astype(BF16)
    return jnp.dot(a, wd, preferred_element_type=F32)


def _ffn_kernel(x_ref, h_ref, p_ref, wg_ref, wu_ref, wd_ref, npl_ref, wpg_ref, wpl_ref, o_ref):
    f = pl.program_id(1)

    @pl.when(f == 0)
    def _():
        o_ref[...] = h_ref[...]

    o_ref[...] += _swiglu(x_ref[...], wg_ref[...], wu_ref[...], wd_ref[...])

    @pl.when(f == pl.num_programs(1) - 1)
    def _():
        o_ref[...] = _ple(o_ref[...], p_ref, npl_ref, wpg_ref, wpl_ref)


def _ffn(x, h, p, wg, wu, wd, npl, wpg, wpl, tm, tf):
    m = x.shape[0]
    row = lambda i, f: (i, 0)
    fixed = lambda i, f: (0, 0)
    return pl.pallas_call(
        _ffn_kernel,
        grid=(m // tm, D_FF // tf),
        in_specs=[
            pl.BlockSpec((tm, D_MODEL), row),
            pl.BlockSpec((tm, D_MODEL), row),
            pl.BlockSpec((tm, D_PLE), row),
            pl.BlockSpec((D_MODEL, tf), lambda i, f: (0, f)),
            pl.BlockSpec((D_MODEL, tf), lambda i, f: (0, f)),
            pl.BlockSpec((tf, D_MODEL), lambda i, f: (f, 0)),
            pl.BlockSpec((1, D_MODEL), fixed),
            pl.BlockSpec((D_MODEL, D_MODEL), fixed),
            pl.BlockSpec((D_PLE, D_MODEL), fixed),
        ],
        out_specs=pl.BlockSpec((tm, D_MODEL), row),
        out_shape=jax.ShapeDtypeStruct((m, D_MODEL), F32),
        compiler_params=pltpu.CompilerParams(
            dimension_semantics=("arbitrary", "arbitrary"), vmem_limit_bytes=VMEM_LIMIT),
        name="ffn",
    )(x, h, p, wg, wu, wd, npl, wpg, wpl)


def _row_copy(src_ref, src_row, dst_ref, dst_row, sem):
    return pltpu.make_async_copy(src_ref.at[pl.ds(src_row, 1)], dst_ref.at[pl.ds(dst_row, 1)], sem)


def _dispatch_kernel(zstart_ref, zflag_ref, dest_ref, x_ref, xs_ref, zero_ref, zsem, sem, *, tm, tg):
    i = pl.program_id(0)

    @pl.when(i == 0)
    def _():
        zero_ref[...] = jnp.zeros_like(zero_ref)
        def fill(e):
            return pltpu.make_async_copy(zero_ref, xs_ref.at[pl.ds(pl.multiple_of(zstart_ref[e], tg), tg)], zsem)

        for e in range(N_FILL):
            @pl.when(zflag_ref[e] > 0)
            def _():
                fill(e).start()
        for e in range(N_FILL):
            @pl.when(zflag_ref[e] > 0)
            def _():
                fill(e).wait()

    def issue(r, carry):
        for s in range(TOP_K):
            _row_copy(x_ref, r, xs_ref, dest_ref[0, 0, TOP_K * r + s], sem).start()
        return carry

    lax.fori_loop(0, tm, issue, 0, unroll=DMA_UNROLL)

    for s in range(TOP_K):
        pltpu.make_async_copy(x_ref, xs_ref.at[pl.ds(0, tm)], sem).wait()


def _dispatch(x, dest, zstart, zflag, n_rows, tm, tg):
    m = x.shape[0]
    return pl.pallas_call(
        functools.partial(_dispatch_kernel, tm=tm, tg=tg),
        grid_spec=pltpu.PrefetchScalarGridSpec(
            num_scalar_prefetch=2,
            grid=(m // tm,),
            in_specs=[
                pl.BlockSpec((1, 1, TOP_K * tm), lambda i, zs, zf: (i, 0, 0), memory_space=pltpu.SMEM),
                pl.BlockSpec((tm, D_MODEL), lambda i, zs, zf: (i, 0)),
            ],
            out_specs=pl.BlockSpec(memory_space=pl.ANY),
            scratch_shapes=[
                pltpu.VMEM((tg, D_MODEL), F32),
                pltpu.SemaphoreType.DMA(()),
                pltpu.SemaphoreType.DMA(()),
            ],
        ),
        out_shape=jax.ShapeDtypeStruct((n_rows, D_MODEL), F32),
        compiler_params=pltpu.CompilerParams(
            dimension_semantics=("arbitrary",), vmem_limit_bytes=VMEM_LIMIT),
        name="moe_dispatch",
    )(zstart, zflag, dest.reshape(m // tm, 1, TOP_K * tm), x)


def _moe_ffn_kernel(te_ref, nused_ref, x_ref, wg_ref, wu_ref, wd_ref, o_ref):
    i = pl.program_id(0)
    f = pl.program_id(1)

    @pl.when((i >= nused_ref[0]) & (f == 0))
    def _():
        o_ref[...] = jnp.zeros_like(o_ref)

    @pl.when(i < nused_ref[0])
    def _():
        y = _swiglu(x_ref[...].astype(BF16), wg_ref[0], wu_ref[0], wd_ref[0])

        @pl.when(f == 0)
        def _():
            o_ref[...] = y

        @pl.when(f > 0)
        def _():
            o_ref[...] += y


def _moe_ffn(xs, te, nused, wg, wu, wd, tg, tf):
    n_rows = xs.shape[0]
    nf = D_FF // tf
    tile = lambda i, nu: jnp.minimum(i, nu[0] - 1)
    fstep = lambda i, f, nu: jnp.where(i < nu[0], f, nf - 1)
    return pl.pallas_call(
        _moe_ffn_kernel,
        grid_spec=pltpu.PrefetchScalarGridSpec(
            num_scalar_prefetch=2,
            grid=(n_rows // tg, nf),
            in_specs=[
                pl.BlockSpec((tg, D_MODEL), lambda i, f, te, nu: (tile(i, nu), 0)),
                pl.BlockSpec((1, D_MODEL, tf), lambda i, f, te, nu: (te[tile(i, nu)], 0, fstep(i, f, nu))),
                pl.BlockSpec((1, D_MODEL, tf), lambda i, f, te, nu: (te[tile(i, nu)], 0, fstep(i, f, nu))),
                pl.BlockSpec((1, tf, D_MODEL), lambda i, f, te, nu: (te[tile(i, nu)], fstep(i, f, nu), 0)),
            ],
            out_specs=pl.BlockSpec((tg, D_MODEL), lambda i, f, te, nu: (i, 0)),
        ),
        out_shape=jax.ShapeDtypeStruct((n_rows, D_MODEL), F32),
        compiler_params=pltpu.CompilerParams(
            dimension_semantics=("arbitrary", "arbitrary"), vmem_limit_bytes=VMEM_LIMIT),
        name="moe_ffn",
    )(te, nused, xs, wg, wu, wd)


def _combine_kernel(dest_ref, route_ref, h_ref, p_ref, npl_ref, wpg_ref, wpl_ref, ys_ref, o_ref,
                    buf_ref, sem, *, tm):
    def issue(r, carry):
        for s in range(TOP_K):
            _row_copy(ys_ref, dest_ref[0, 0, TOP_K * r + s], buf_ref.at[s], r, sem).start()
        return carry

    lax.fori_loop(0, tm, issue, 0, unroll=DMA_UNROLL)

    for s in range(TOP_K):
        pltpu.make_async_copy(ys_ref.at[pl.ds(0, tm)], buf_ref.at[s], sem).wait()

    g1 = route_ref[:, 2:3]
    g2 = route_ref[:, 3:4]
    h2 = h_ref[...] + (g1 * buf_ref[0] + g2 * buf_ref[1])
    o_ref[...] = _ple(h2, p_ref, npl_ref, wpg_ref, wpl_ref)


def _combine(ys, dest, route, h, p, npl, wpg, wpl, tm):
    m = h.shape[0]
    row = lambda i: (i, 0)
    fixed = lambda i: (0, 0)
    return pl.pallas_call(
        functools.partial(_combine_kernel, tm=tm),
        grid=(m // tm,),
        in_specs=[
            pl.BlockSpec((1, 1, TOP_K * tm), lambda i: (i, 0, 0), memory_space=pltpu.SMEM),
            pl.BlockSpec((tm, LANES), row),
            pl.BlockSpec((tm, D_MODEL), row),
            pl.BlockSpec((tm, D_PLE), row),
            pl.BlockSpec((1, D_MODEL), fixed),
            pl.BlockSpec((D_MODEL, D_MODEL), fixed),
            pl.BlockSpec((D_PLE, D_MODEL), fixed),
            pl.BlockSpec(memory_space=pl.ANY),
        ],
        out_specs=pl.BlockSpec((tm, D_MODEL), row),
        out_shape=jax.ShapeDtypeStruct((m, D_MODEL), F32),
        scratch_shapes=[
            pltpu.VMEM((TOP_K, tm, D_MODEL), F32),
            pltpu.SemaphoreType.DMA(()),
        ],
        compiler_params=pltpu.CompilerParams(
            dimension_semantics=("arbitrary",), vmem_limit_bytes=VMEM_LIMIT),
        name="moe_combine",
    )(dest.reshape(m // tm, 1, TOP_K * tm), route, h, p, npl, wpg, wpl, ys)


def _routing(route, tg):
    m = route.shape[0]
    e = route[:, :TOP_K].astype(jnp.int32)
    sel = jnp.sum((e[:, :, None] == jnp.arange(N_EXPERTS, dtype=jnp.int32)).astype(jnp.int32), axis=1)
    csum = jnp.cumsum(sel, axis=0)
    rank = csum - sel
    counts = csum[-1]
    padded = (counts + tg - 1) // tg * tg
    ends = jnp.cumsum(padded)
    offs = ends - padded
    dest = offs[e] + jnp.take_along_axis(rank, e, axis=1)
    n_tiles = TOP_K * m // tg + N_EXPERTS
    nused = ends[-1] // tg
    tile_ids = jnp.minimum(jnp.arange(n_tiles, dtype=jnp.int32), nused - 1)
    te = jnp.sum((tile_ids[:, None] >= (ends // tg)[None, :]).astype(jnp.int32), axis=1)
    te = jnp.minimum(te, N_EXPERTS - 1)
    slack = nused + jnp.arange(N_FILL - N_EXPERTS, dtype=jnp.int32)
    zstart = jnp.concatenate([jnp.maximum(ends - tg, 0), jnp.minimum(slack, n_tiles - 1) * tg])
    zflag = jnp.concatenate([counts > 0, slack < n_tiles]).astype(jnp.int32)
    return (dest.reshape(-1).astype(jnp.int32), te.astype(jnp.int32), nused.reshape(1).astype(jnp.int32),
            zstart.astype(jnp.int32), zflag, n_tiles * tg)


def _moe(hn, h1, p, route, wg, wu, wd, npl, wpg, wpl, tm, tg, tf):
    dest, te, nused, zstart, zflag, n_rows = _routing(route, tg)
    xs = _dispatch(hn, dest, zstart, zflag, n_rows, tm, tg)
    ys = _moe_ffn(xs, te, nused, wg, wu, wd, tg, tf)
    return _combine(ys, dest, route, h1, p, npl, wpg, wpl, tm)


def _alibi_bias():
    slopes = jnp.exp2(-8.0 * jnp.arange(1, N_HEADS + 1, dtype=F32) / N_HEADS)
    qi = jnp.arange(CHUNK, dtype=jnp.int32)[None, :]
    kj = jnp.arange(BAND, dtype=jnp.int32)[:, None]
    dist = jnp.abs(qi + WINDOW - kj).astype(F32)
    bias = slopes[:, None, None] * dist[None]
    bias = bias.reshape(N_KV_HEADS, GROUP, BAND, CHUNK)
    return jnp.transpose(bias, (0, 2, 1, 3)).reshape(N_KV_HEADS, BAND, GROUP * CHUNK)


def _head_sum_matrix():
    head = jnp.arange(QK_WIDTH, dtype=jnp.int32) // HEAD_DIM
    return (head[:, None] == head[None, :]).astype(BF16)


def _layer_params(l, norm_mix, w_in, q_gain, k_gain, attn_sinks, conv_w, conv_b, conv_ln_g, conv_ln_b,
                  out_norm_attn, out_norm_conv, w_out, norm_ffn, router_w, router_b, ple_norm,
                  w_ple_gate, w_ple):
    scale = HEAD_DIM ** -0.5
    prm = {
        "norm_mix": norm_mix[l][None, :],
        "w_in": w_in[l].astype(BF16),
        "qk_gain": jnp.concatenate([jnp.tile(q_gain[l] * scale, N_HEADS), jnp.tile(k_gain[l], N_KV_HEADS)])[None, :],
        "sinks": jnp.broadcast_to(attn_sinks[l].reshape(N_KV_HEADS, 1, GROUP, 1),
                                  (N_KV_HEADS, 1, GROUP, CHUNK)).reshape(N_KV_HEADS, 1, GROUP * CHUNK),
        "conv_w": jnp.pad(conv_w[l], ((0, CONV_PAD - CONV_K), (0, 0))),
        "conv_b": conv_b[l][None, :],
        "cln_g": conv_ln_g[l][None, :],
        "cln_b": conv_ln_b[l][None, :],
        "on_attn": out_norm_attn[l][None, :],
        "on_conv": out_norm_conv[l][None, :],
        "w_out": w_out[l].astype(BF16),
        "norm_ffn": norm_ffn[l][None, :],
        "ple_norm": ple_norm[l][None, :],
        "w_ple_gate": w_ple_gate[l].astype(BF16),
        "w_ple": w_ple[l].astype(BF16),
        "router_w": None,
        "router_b": None,
    }
    if l % 2 == 1:
        i = l // 2
        rw = jnp.pad(router_w[i], ((0, 0), (0, LANES - N_EXPERTS)))
        rw_hi = rw.astype(BF16)
        rw_lo = (rw - rw_hi.astype(F32)).astype(BF16)
        prm["router_w"] = jnp.concatenate([rw_hi, rw_lo], axis=1)
        prm["router_b"] = jnp.pad(router_b[i], (0, LANES - N_EXPERTS))[None, :]
    return prm


def _trunk(x, p, cache_k, cache_v, state_conv, layers, ffn_w, bias, hsum, *, tm, tq):
    nseq, seq, _ = x.shape
    m = nseq * seq
    has_past = cache_k is not None
    h = x.reshape(m, D_MODEL)
    win_k, win_v, convs = [], [], []
    for l, prm in enumerate(layers):
        q, kv, u = _in_proj(h, prm["norm_mix"], prm["w_in"], prm["qk_gain"], hsum, tm)
        if has_past:
            win = cache_k.shape[2]
            kv_prev = jnp.concatenate([cache_k[l].reshape(nseq * win, KV_WIDTH),
                                       cache_v[l].reshape(nseq * win, KV_WIDTH)], axis=1)
            u_prev = jnp.pad(state_conv[l], ((0, 0), (CONV_PAD - (CONV_K - 1), 0), (0, 0))).reshape(
                nseq * CONV_PAD, CONV_CH)
        else:
            kv_prev, u_prev = kv, u
        outs = _mixer(q, kv, kv_prev, u, u_prev, h, dict(prm, bias=bias), nseq=nseq, tq=tq, has_past=has_past)
        h1, hn = outs[0], outs[1]
        wg, wu, wd = ffn_w[l]
        pl_ = p[l].reshape(m, D_PLE)
        if len(outs) == 3:
            h = _moe(hn, h1, pl_, outs[2], wg, wu, wd, prm["ple_norm"], prm["w_ple_gate"], prm["w_ple"],
                     tm, tm, D_FF // 2)
        else:
            h = _ffn(hn, h1, pl_, wg, wu, wd, prm["ple_norm"], prm["w_ple_gate"], prm["w_ple"], tm, D_FF // 2)
        kv3 = kv.reshape(nseq, seq, 2 * KV_WIDTH)
        u3 = u.reshape(nseq, seq, CONV_CH)
        if has_past:
            kv3 = jnp.concatenate([kv_prev.reshape(nseq, win, 2 * KV_WIDTH), kv3], axis=1)[:, -win:]
            u3 = jnp.concatenate([state_conv[l], u3], axis=1)
        else:
            kv3 = kv3[:, seq - WINDOW:]
        win_k.append(kv3[..., :KV_WIDTH].reshape(nseq, -1, N_KV_HEADS, HEAD_DIM))
        win_v.append(kv3[..., KV_WIDTH:].reshape(nseq, -1, N_KV_HEADS, HEAD_DIM))
        convs.append(u3[:, -(CONV_K - 1):])
    return h.reshape(nseq, seq, D_MODEL), jnp.stack(win_k), jnp.stack(win_v), jnp.stack(convs)


def kernel(x_prompt, x_sample, p_prompt, p_sample, cache_k, cache_v, state_conv, norm_mix, w_in, q_gain, k_gain, attn_sinks, conv_w, conv_b, conv_ln_g, conv_ln_b, out_norm_attn, out_norm_conv, w_out, norm_ffn, ffn_gate, ffn_up, ffn_down, router_w, router_b, moe_gate, moe_up, moe_down, ple_norm, w_ple_gate, w_ple):
    depth = w_in.shape[0]
    layers = [
        _layer_params(l, norm_mix, w_in, q_gain, k_gain, attn_sinks, conv_w, conv_b, conv_ln_g, conv_ln_b,
                      out_norm_attn, out_norm_conv, w_out, norm_ffn, router_w, router_b, ple_norm,
                      w_ple_gate, w_ple)
        for l in range(depth)
    ]
    ffn_w = []
    for l in range(depth):
        i = l // 2
        if l % 2 == 0:
            ffn_w.append((ffn_gate[i].astype(BF16), ffn_up[i].astype(BF16), ffn_down[i].astype(BF16)))
        else:
            ffn_w.append((moe_gate[i].astype(BF16), moe_up[i].astype(BF16), moe_down[i].astype(BF16)))
    bias = _alibi_bias()
    hsum = _head_sum_matrix()
    tm_p = min(512, x_prompt.shape[0] * x_prompt.shape[1])
    tq_p = min(512, x_prompt.shape[1])
    y_p, wk_p, wv_p, cv_p = _trunk(x_prompt, p_prompt, None, None, None, layers, ffn_w, bias, hsum,
                                   tm=tm_p, tq=tq_p)
    tm_s = min(512, x_sample.shape[0] * x_sample.shape[1])
    y_s, wk_s, wv_s, cv_s = _trunk(x_sample, p_sample, cache_k, cache_v, state_conv, layers, ffn_w, bias, hsum,
                                   tm=tm_s, tq=x_sample.shape[1])
    return (y_p, y_s, wk_p, wv_p, cv_p, wk_s, wv_s, cv_s)
```

```python
import functools

import jax
import jax.numpy as jnp
from jax import lax
from jax.experimental import pallas as pl
from jax.experimental.pallas import tpu as pltpu

D_MODEL = 1024
CHUNK = 64
WINDOW = 128
N_HEADS = 8
N_KV_HEADS = 2
HEAD_DIM = 64
GROUP = N_HEADS // N_KV_HEADS
ATTN_WIDTH = N_HEADS * HEAD_DIM
KV_WIDTH = N_KV_HEADS * HEAD_DIM
QK_WIDTH = ATTN_WIDTH + KV_WIDTH
CONV_CH = 512
CONV_K = 31
SUBLANES = 8
CONV_PAD = 32
CONV_ROWS = 64
CONV_ACCS = 2
D_IN = ATTN_WIDTH + 2 * KV_WIDTH + 2 * CONV_CH
BAND = WINDOW + CHUNK
D_FF = 2816
N_EXPERTS = 8
TOP_K = 2
N_FILL = 2 * N_EXPERTS
D_PLE = 256
DMA_UNROLL = 8
EPS = 1e-6
LANES = 128

F32 = jnp.float32
BF16 = jnp.bfloat16

VMEM_LIMIT = 56 * 1024 * 1024


def _rms(x, g):
    return x * lax.rsqrt(jnp.mean(x * x, axis=-1, keepdims=True) + EPS) * g


def _in_proj_kernel(h_ref, nrm_ref, w_ref, gain_ref, hsum_ref, q_ref, kv_ref, u_ref):
    xn = _rms(h_ref[...], nrm_ref[...])
    z = jnp.dot(xn.astype(BF16), w_ref[...], preferred_element_type=F32)
    qk = z[:, :QK_WIDTH]
    ss = jnp.dot((qk * qk).astype(BF16), hsum_ref[...], preferred_element_type=F32)
    qkn = qk * lax.rsqrt(ss * (1.0 / HEAD_DIM) + EPS) * gain_ref[...]
    q_ref[...] = qkn[:, :ATTN_WIDTH].astype(q_ref.dtype)
    kv_ref[:, :KV_WIDTH] = qkn[:, ATTN_WIDTH:]
    kv_ref[:, KV_WIDTH:] = z[:, QK_WIDTH:QK_WIDTH + KV_WIDTH]
    a = z[:, QK_WIDTH + KV_WIDTH:QK_WIDTH + KV_WIDTH + CONV_CH]
    gl = z[:, QK_WIDTH + KV_WIDTH + CONV_CH:]
    u_ref[...] = a * jax.nn.sigmoid(gl)


def _in_proj(h, nrm, w_in, qk_gain, hsum, tm):
    m = h.shape[0]
    row = lambda i: (i, 0)
    fixed = lambda i: (0, 0)
    return pl.pallas_call(
        _in_proj_kernel,
        grid=(m // tm,),
        in_specs=[
            pl.BlockSpec((tm, D_MODEL), row),
            pl.BlockSpec((1, D_MODEL), fixed),
            pl.BlockSpec((D_MODEL, D_IN), fixed),
            pl.BlockSpec((1, QK_WIDTH), fixed),
            pl.BlockSpec((QK_WIDTH, QK_WIDTH), fixed),
        ],
        out_specs=[
            pl.BlockSpec((tm, ATTN_WIDTH), row),
            pl.BlockSpec((tm, 2 * KV_WIDTH), row),
            pl.BlockSpec((tm, CONV_CH), row),
        ],
        out_shape=[
            jax.ShapeDtypeStruct((m, ATTN_WIDTH), BF16),
            jax.ShapeDtypeStruct((m, 2 * KV_WIDTH), F32),
            jax.ShapeDtypeStruct((m, CONV_CH), F32),
        ],
        compiler_params=pltpu.CompilerParams(
            dimension_semantics=("arbitrary",), vmem_limit_bytes=VMEM_LIMIT),
        name="in_proj",
    )(h, nrm, w_in, qk_gain, hsum)


def _mixer_kernel(q_ref, kv_ref, kvp_ref, u_ref, up_ref, h_ref, bias_ref, sink_ref,
                  cw_ref, cb_ref, lng_ref, lnb_ref, ona_ref, onc_ref, wout_ref, nffn_ref,
                  *rest, tq, has_past, with_router):
    if with_router:
        (rw_ref, rb_ref, ltri_ref, h1_ref, hn_ref, route_ref, routet_ref, count_ref,
         kvx_ref, ux_ref, oa_ref, cv_ref, cnt_ref) = rest
    else:
        h1_ref, hn_ref, kvx_ref, ux_ref, oa_ref, cv_ref = rest
    t = pl.program_id(1)
    nch = tq // CHUNK

    kvx_ref[:WINDOW, :] = kvp_ref[...].astype(BF16)
    kvx_ref[WINDOW:, :] = kv_ref[...].astype(BF16)

    for c in range(nch):
        qc = q_ref[c * CHUNK:(c + 1) * CHUNK, :]
        kvb = kvx_ref[c * CHUNK:c * CHUNK + BAND, :]
        if not has_past and c < WINDOW // CHUNK:
            kpos = lax.broadcasted_iota(jnp.int32, (BAND, 1), 0) + (t * tq + c * CHUNK - WINDOW)
            valid = kpos >= 0
        else:
            valid = None
        outs = []
        for j in range(N_KV_HEADS):
            qs = jnp.concatenate(
                [qc[:, (j * GROUP + g) * HEAD_DIM:(j * GROUP + g + 1) * HEAD_DIM] for g in range(GROUP)],
                axis=0)
            kj = kvb[:, j * HEAD_DIM:(j + 1) * HEAD_DIM]
            vj = kvb[:, KV_WIDTH + j * HEAD_DIM:KV_WIDTH + (j + 1) * HEAD_DIM]
            s = lax.dot_general(kj, qs, (((1,), (1,)), ((), ())), preferred_element_type=F32)
            s = s - bias_ref[j]
            if valid is not None:
                s = jnp.where(valid, s, -jnp.inf)
            sink = sink_ref[j]
            mx = jnp.maximum(jnp.max(s, axis=0, keepdims=True), sink)
            e = jnp.exp(s - mx)
            den = jnp.sum(e, axis=0, keepdims=True) + jnp.exp(sink - mx)
            prob = (e * (1.0 / den)).astype(BF16)
            o = lax.dot_general(prob, vj, (((0,), (0,)), ((), ())), preferred_element_type=F32)
            outs.extend(o[g * CHUNK:(g + 1) * CHUNK, :] for g in range(GROUP))
        oa_ref[c * CHUNK:(c + 1) * CHUNK, :] = jnp.concatenate(outs, axis=1)

    if has_past:
        ux_ref[0, :CONV_PAD, :] = up_ref[...]
    else:
        ux_ref[0, :CONV_PAD, :] = jnp.where(t > 0, up_ref[...], 0.0)
    ux_ref[0, CONV_PAD:, :] = u_ref[...]
    shifted_rows = tq + CONV_PAD - SUBLANES
    for r in range(1, SUBLANES):
        ux_ref[r, :shifted_rows, :] = ux_ref[0, r:r + shifted_rows, :]
    lead = CONV_PAD - (CONV_K - 1)

    for cg in range(CONV_CH // LANES):
        lanes = slice(cg * LANES, (cg + 1) * LANES)
        taps = [cw_ref[k:k + 1, lanes] for k in range(CONV_K)]

        def conv_rows(i, carry, lanes=lanes, taps=taps):
            r0 = pl.multiple_of(i * CONV_ROWS, CONV_ROWS)
            accs = [None] * CONV_ACCS
            for r in range(SUBLANES):
                steps = [(k, (lead + k) // SUBLANES) for k in range(CONV_K) if (lead + k) % SUBLANES == r]
                slab = ux_ref[r, pl.ds(r0, CONV_ROWS + steps[-1][1] * SUBLANES), lanes]
                for k, a in steps:
                    term = slab[a * SUBLANES:a * SUBLANES + CONV_ROWS, :] * taps[k]
                    accs[k % CONV_ACCS] = term if accs[k % CONV_ACCS] is None else accs[k % CONV_ACCS] + term
            cv_ref[pl.ds(r0, CONV_ROWS), lanes] = functools.reduce(lambda x, y: x + y, accs) + cb_ref[:, lanes]
            return carry

        lax.fori_loop(0, tq // CONV_ROWS, conv_rows, 0)

    cv = cv_ref[...]
    mu = jnp.mean(cv, axis=-1, keepdims=True)
    xc = cv - mu
    ln = xc * lax.rsqrt(jnp.mean(xc * xc, axis=-1, keepdims=True) + EPS) * lng_ref[...] + lnb_ref[...]
    oc = ln * jax.nn.sigmoid(ln)
    cn = _rms(oc, onc_ref[...]).astype(BF16)

    an = _rms(oa_ref[...], ona_ref[...]).astype(BF16)
    mixed = (jnp.dot(an, wout_ref[:ATTN_WIDTH, :], preferred_element_type=F32)
             + jnp.dot(cn, wout_ref[ATTN_WIDTH:, :], preferred_element_type=F32))
    h1 = h_ref[...] + mixed
    h1_ref[...] = h1
    hn = _rms(h1, nffn_ref[...])
    hn_ref[...] = hn.astype(hn_ref.dtype)

    if with_router:
        hi = hn.astype(BF16)
        lo = (hn - hi.astype(F32)).astype(BF16)
        part = jnp.dot(hi, rw_ref[...], preferred_element_type=F32)
        logits = (part[:, :LANES] + part[:, LANES:]
                  + jnp.dot(lo, rw_ref[:, :LANES], preferred_element_type=F32) + rb_ref[...])
        col = lax.broadcasted_iota(jnp.int32, logits.shape, 1)
        logits = jnp.where(col < N_EXPERTS, logits, -jnp.inf)
        m1 = jnp.max(logits, axis=-1, keepdims=True)
        i1 = jnp.min(jnp.where(logits == m1, col, LANES), axis=-1, keepdims=True)
        rest_l = jnp.where(col == i1, -jnp.inf, logits)
        m2 = jnp.max(rest_l, axis=-1, keepdims=True)
        i2 = jnp.min(jnp.where(rest_l == m2, col, LANES), axis=-1, keepdims=True)
        e2 = jnp.exp(m2 - m1)
        den = 1.0 + e2

        @pl.when((pl.program_id(0) == 0) & (t == 0))
        def _():
            cnt_ref[...] = jnp.zeros_like(cnt_ref)

        sel = ((col == i1) | (col == i2)).astype(F32)
        ahead = jnp.dot(ltri_ref[...], sel.astype(BF16), preferred_element_type=F32) + cnt_ref[...]
        r1 = jnp.sum(jnp.where(col == i1, ahead, 0.0), axis=-1, keepdims=True)
        r2 = jnp.sum(jnp.where(col == i2, ahead, 0.0), axis=-1, keepdims=True)
        cnt_ref[...] += jnp.sum(sel, axis=0, keepdims=True)
        count_ref[...] = jnp.broadcast_to(cnt_ref[...], count_ref.shape)

        route = jnp.where(
            col == 0, i1.astype(F32),
            jnp.where(col == 1, i2.astype(F32),
                      jnp.where(col == 2, 1.0 / den,
                                jnp.where(col == 3, e2 / den,
                                          jnp.where(col == 4, r1, jnp.where(col == 5, r2, 0.0))))))
        route_ref[...] = route
        routet_ref[0] = route.T[:SUBLANES, :]


def _mixer(q, kv, kv_prev, u, u_prev, h, prm, *, nseq, tq, has_past):
    m = h.shape[0]
    seq = m // nseq
    nt = seq // tq
    with_router = prm["router_w"] is not None
    cur = lambda b, t: (b * nt + t, 0)
    fixed2 = lambda b, t: (0, 0)
    fixed3 = lambda b, t: (0, 0, 0)
    if has_past:
        kvp_map = lambda b, t: (b, 0)
        up_map = lambda b, t: (b, 0)
    else:
        kvp_map = lambda b, t: (jnp.maximum(b * (seq // WINDOW) + t * (tq // WINDOW) - 1, 0), 0)
        up_map = lambda b, t: (jnp.maximum(b * (seq // CONV_PAD) + t * (tq // CONV_PAD) - 1, 0), 0)
    in_specs = [
        pl.BlockSpec((tq, ATTN_WIDTH), cur),
        pl.BlockSpec((tq, 2 * KV_WIDTH), cur),
        pl.BlockSpec((WINDOW, 2 * KV_WIDTH), kvp_map),
        pl.BlockSpec((tq, CONV_CH), cur),
        pl.BlockSpec((CONV_PAD, CONV_CH), up_map),
        pl.BlockSpec((tq, D_MODEL), cur),
        pl.BlockSpec((N_KV_HEADS, BAND, GROUP * CHUNK), fixed3),
        pl.BlockSpec((N_KV_HEADS, 1, GROUP * CHUNK), fixed3),
        pl.BlockSpec((CONV_PAD, CONV_CH), fixed2),
        pl.BlockSpec((1, CONV_CH), fixed2),
        pl.BlockSpec((1, CONV_CH), fixed2),
        pl.BlockSpec((1, CONV_CH), fixed2),
        pl.BlockSpec((1, ATTN_WIDTH), fixed2),
        pl.BlockSpec((1, CONV_CH), fixed2),
        pl.BlockSpec((D_MODEL, D_MODEL), fixed2),
        pl.BlockSpec((1, D_MODEL), fixed2),
    ]
    args = [q, kv, kv_prev, u, u_prev, h, prm["bias"], prm["sinks"], prm["conv_w"], prm["conv_b"],
            prm["cln_g"], prm["cln_b"], prm["on_attn"], prm["on_conv"], prm["w_out"], prm["norm_ffn"]]
    out_specs = [pl.BlockSpec((tq, D_MODEL), cur), pl.BlockSpec((tq, D_MODEL), cur)]
    out_shape = [jax.ShapeDtypeStruct((m, D_MODEL), F32),
                 jax.ShapeDtypeStruct((m, D_MODEL), F32 if with_router else BF16)]
    scratch = [
        pltpu.VMEM((WINDOW + tq, 2 * KV_WIDTH), BF16),
        pltpu.VMEM((SUBLANES, CONV_PAD + tq, CONV_CH), F32),
        pltpu.VMEM((tq, ATTN_WIDTH), F32),
        pltpu.VMEM((tq, CONV_CH), F32),
    ]
    if with_router:
        ltri = (jnp.arange(tq, dtype=jnp.int32)[:, None] > jnp.arange(tq, dtype=jnp.int32)[None, :]).astype(BF16)
        in_specs += [pl.BlockSpec((D_MODEL, 2 * LANES), fixed2), pl.BlockSpec((1, LANES), fixed2),
                     pl.BlockSpec((tq, tq), fixed2)]
        args += [prm["router_w"], prm["router_b"], ltri]
        out_specs += [pl.BlockSpec((tq, LANES), cur),
                      pl.BlockSpec((1, SUBLANES, tq), lambda b, t: (b * nt + t, 0, 0)),
                      pl.BlockSpec((SUBLANES, LANES), fixed2)]
        out_shape += [jax.ShapeDtypeStruct((m, LANES), F32),
                      jax.ShapeDtypeStruct((nseq * nt, SUBLANES, tq), F32),
                      jax.ShapeDtypeStruct((SUBLANES, LANES), F32)]
        scratch.append(pltpu.VMEM((1, LANES), F32))
    return pl.pallas_call(
        functools.partial(_mixer_kernel, tq=tq, has_past=has_past, with_router=with_router),
        grid=(nseq, nt),
        in_specs=in_specs,
        out_specs=out_specs,
        out_shape=out_shape,
        scratch_shapes=scratch,
        compiler_params=pltpu.CompilerParams(
            dimension_semantics=("arbitrary", "arbitrary"), vmem_limit_bytes=VMEM_LIMIT),
        name="mixer_past" if has_past else "mixer",
    )(*args)


def _ple(h2, p_ref, npl_ref, wpg_ref, wpl_ref):
    r = _rms(h2, npl_ref[...]).astype(BF16)
    gate = jax.nn.sigmoid(jnp.dot(r, wpg_ref[...], preferred_element_type=F32))
    pe = jnp.dot(p_ref[...].astype(BF16), wpl_ref[...], preferred_element_type=F32)
    return h2 + gate * pe


def _swiglu(x, wg, wu, wd):
    g = jnp.dot(x, wg, preferred_element_type=F32)
    u = jnp.dot(x, wu, preferred_element_type=F32)
    a = (g * jax.nn.sigmoid(g) * u).astype(BF16)
    return jnp.dot(a, wd, preferred_element_type=F32)


def _ffn_kernel(x_ref, h_ref, p_ref, wg_ref, wu_ref, wd_ref, npl_ref, wpg_ref, wpl_ref, o_ref):
    f = pl.program_id(1)

    @pl.when(f == 0)
    def _():
        o_ref[...] = h_ref[...]

    o_ref[...] += _swiglu(x_ref[...], wg_ref[...], wu_ref[...], wd_ref[...])

    @pl.when(f == pl.num_programs(1) - 1)
    def _():
        o_ref[...] = _ple(o_ref[...], p_ref, npl_ref, wpg_ref, wpl_ref)


def _ffn(x, h, p, layer, wg, wu, wd, npl, wpg, wpl, tm, tf):
    m = x.shape[0]
    row = lambda i, f: (i, 0)
    fixed = lambda i, f: (0, 0)
    return pl.pallas_call(
        _ffn_kernel,
        grid=(m // tm, D_FF // tf),
        in_specs=[
            pl.BlockSpec((tm, D_MODEL), row),
            pl.BlockSpec((tm, D_MODEL), row),
            pl.BlockSpec((None, tm, D_PLE), lambda i, f: (layer, i, 0)),
            pl.BlockSpec((D_MODEL, tf), lambda i, f: (0, f)),
            pl.BlockSpec((D_MODEL, tf), lambda i, f: (0, f)),
            pl.BlockSpec((tf, D_MODEL), lambda i, f: (f, 0)),
            pl.BlockSpec((1, D_MODEL), fixed),
            pl.BlockSpec((D_MODEL, D_MODEL), fixed),
            pl.BlockSpec((D_PLE, D_MODEL), fixed),
        ],
        out_specs=pl.BlockSpec((tm, D_MODEL), row),
        out_shape=jax.ShapeDtypeStruct((m, D_MODEL), F32),
        compiler_params=pltpu.CompilerParams(
            dimension_semantics=("arbitrary", "arbitrary"), vmem_limit_bytes=VMEM_LIMIT),
        name="ffn",
    )(x, h, p, wg, wu, wd, npl, wpg, wpl)


def _row_copy(src_ref, src_row, dst_ref, dst_row, sem):
    return pltpu.make_async_copy(src_ref.at[pl.ds(src_row, 1)], dst_ref.at[pl.ds(dst_row, 1)], sem)


def _dispatch_kernel(zstart_ref, zflag_ref, dest_ref, x_ref, xs_ref, zero_ref, zsem, sem, *, tm, tg):
    i = pl.program_id(0)

    @pl.when(i == 0)
    def _():
        zero_ref[...] = jnp.zeros_like(zero_ref)
        def fill(e):
            return pltpu.make_async_copy(zero_ref, xs_ref.at[pl.ds(pl.multiple_of(zstart_ref[e], tg), tg)], zsem)

        for e in range(N_FILL):
            @pl.when(zflag_ref[e] > 0)
            def _():
                fill(e).start()
        for e in range(N_FILL):
            @pl.when(zflag_ref[e] > 0)
            def _():
                fill(e).wait()

    def issue(r, carry):
        for s in range(TOP_K):
            _row_copy(x_ref, r, xs_ref, dest_ref[0, s, r], sem).start()
        return carry

    lax.fori_loop(0, tm, issue, 0, unroll=DMA_UNROLL)

    for s in range(TOP_K):
        pltpu.make_async_copy(x_ref, xs_ref.at[pl.ds(0, tm)], sem).wait()


def _dispatch(x, dest, zstart, zflag, n_rows, tm, tg):
    m = x.shape[0]
    return pl.pallas_call(
        functools.partial(_dispatch_kernel, tm=tm, tg=tg),
        grid_spec=pltpu.PrefetchScalarGridSpec(
            num_scalar_prefetch=2,
            grid=(m // tm,),
            in_specs=[
                pl.BlockSpec((1, TOP_K, tm), lambda i, zs, zf: (i, 0, 0), memory_space=pltpu.SMEM),
                pl.BlockSpec((tm, D_MODEL), lambda i, zs, zf: (i, 0)),
            ],
            out_specs=pl.BlockSpec(memory_space=pl.ANY),
            scratch_shapes=[
                pltpu.VMEM((tg, D_MODEL), F32),
                pltpu.SemaphoreType.DMA(()),
                pltpu.SemaphoreType.DMA(()),
            ],
        ),
        out_shape=jax.ShapeDtypeStruct((n_rows, D_MODEL), F32),
        compiler_params=pltpu.CompilerParams(
            dimension_semantics=("arbitrary",), vmem_limit_bytes=VMEM_LIMIT),
        name="moe_dispatch",
    )(zstart, zflag, dest, x)


def _moe_ffn_kernel(te_ref, nused_ref, x_ref, wg_ref, wu_ref, wd_ref, o_ref):
    i = pl.program_id(0)
    f = pl.program_id(1)

    @pl.when((i >= nused_ref[0]) & (f == 0))
    def _():
        o_ref[...] = jnp.zeros_like(o_ref)

    @pl.when(i < nused_ref[0])
    def _():
        y = _swiglu(x_ref[...].astype(BF16), wg_ref[0], wu_ref[0], wd_ref[0])

        @pl.when(f == 0)
        def _():
            o_ref[...] = y

        @pl.when(f > 0)
        def _():
            o_ref[...] += y


def _moe_ffn(xs, te, nused, wg, wu, wd, tg, tf):
    n_rows = xs.shape[0]
    nf = D_FF // tf
    tile = lambda i, nu: jnp.minimum(i, nu[0] - 1)
    fstep = lambda i, f, nu: jnp.where(i < nu[0], f, nf - 1)
    return pl.pallas_call(
        _moe_ffn_kernel,
        grid_spec=pltpu.PrefetchScalarGridSpec(
            num_scalar_prefetch=2,
            grid=(n_rows // tg, nf),
            in_specs=[
                pl.BlockSpec((tg, D_MODEL), lambda i, f, te, nu: (tile(i, nu), 0)),
                pl.BlockSpec((1, D_MODEL, tf), lambda i, f, te, nu: (te[tile(i, nu)], 0, fstep(i, f, nu))),
                pl.BlockSpec((1, D_MODEL, tf), lambda i, f, te, nu: (te[tile(i, nu)], 0, fstep(i, f, nu))),
                pl.BlockSpec((1, tf, D_MODEL), lambda i, f, te, nu: (te[tile(i, nu)], fstep(i, f, nu), 0)),
            ],
            out_specs=pl.BlockSpec((tg, D_MODEL), lambda i, f, te, nu: (i, 0)),
        ),
        out_shape=jax.ShapeDtypeStruct((n_rows, D_MODEL), F32),
        compiler_params=pltpu.CompilerParams(
            dimension_semantics=("arbitrary", "arbitrary"), vmem_limit_bytes=VMEM_LIMIT),
        name="moe_ffn",
    )(te, nused, xs, wg, wu, wd)


def _combine_kernel(dest_ref, route_ref, h_ref, p_ref, npl_ref, wpg_ref, wpl_ref, ys_ref, o_ref,
                    buf_ref, sem, *, tm):
    def issue(r, carry):
        for s in range(TOP_K):
            _row_copy(ys_ref, dest_ref[0, s, r], buf_ref.at[s], r, sem).start()
        return carry

    lax.fori_loop(0, tm, issue, 0, unroll=DMA_UNROLL)

    for s in range(TOP_K):
        pltpu.make_async_copy(ys_ref.at[pl.ds(0, tm)], buf_ref.at[s], sem).wait()

    g1 = route_ref[:, 2:3]
    g2 = route_ref[:, 3:4]
    h2 = h_ref[...] + (g1 * buf_ref[0] + g2 * buf_ref[1])
    o_ref[...] = _ple(h2, p_ref, npl_ref, wpg_ref, wpl_ref)


def _combine(ys, dest, route, h, p, layer, npl, wpg, wpl, tm):
    m = h.shape[0]
    row = lambda i: (i, 0)
    fixed = lambda i: (0, 0)
    return pl.pallas_call(
        functools.partial(_combine_kernel, tm=tm),
        grid=(m // tm,),
        in_specs=[
            pl.BlockSpec((1, TOP_K, tm), lambda i: (i, 0, 0), memory_space=pltpu.SMEM),
            pl.BlockSpec((tm, LANES), row),
            pl.BlockSpec((tm, D_MODEL), row),
            pl.BlockSpec((None, tm, D_PLE), lambda i: (layer, i, 0)),
            pl.BlockSpec((1, D_MODEL), fixed),
            pl.BlockSpec((D_MODEL, D_MODEL), fixed),
            pl.BlockSpec((D_PLE, D_MODEL), fixed),
            pl.BlockSpec(memory_space=pl.ANY),
        ],
        out_specs=pl.BlockSpec((tm, D_MODEL), row),
        out_shape=jax.ShapeDtypeStruct((m, D_MODEL), F32),
        scratch_shapes=[
            pltpu.VMEM((TOP_K, tm, D_MODEL), F32),
            pltpu.SemaphoreType.DMA(()),
        ],
        compiler_params=pltpu.CompilerParams(
            dimension_semantics=("arbitrary",), vmem_limit_bytes=VMEM_LIMIT),
        name="moe_combine",
    )(dest, route, h, p, npl, wpg, wpl, ys)


def _routing(route_t, count, m, tm, tg):
    rows = jnp.transpose(route_t, (1, 0, 2)).reshape(SUBLANES, m)
    counts = count[0, :N_EXPERTS].astype(jnp.int32)
    padded = (counts + tg - 1) // tg * tg
    ends = jnp.cumsum(padded)
    offs = ends - padded

    def dest_rows(expert, rank):
        expert = expert.astype(jnp.int32)
        start = sum(jnp.where(expert == k, offs[k], 0) for k in range(N_EXPERTS))
        return (start + rank.astype(jnp.int32)).reshape(m // tm, 1, tm)

    dest = jnp.concatenate([dest_rows(rows[s], rows[2 * TOP_K + s]) for s in range(TOP_K)], axis=1)
    n_tiles = TOP_K * m // tg + N_EXPERTS
    nused = ends[-1] // tg
    tile_ids = jnp.minimum(jnp.arange(n_tiles, dtype=jnp.int32), nused - 1)
    te = jnp.sum((tile_ids[:, None] >= (ends // tg)[None, :]).astype(jnp.int32), axis=1)
    te = jnp.minimum(te, N_EXPERTS - 1)
    slack = nused + jnp.arange(N_FILL - N_EXPERTS, dtype=jnp.int32)
    zstart = jnp.concatenate([jnp.maximum(ends - tg, 0), jnp.minimum(slack, n_tiles - 1) * tg])
    zflag = jnp.concatenate([counts > 0, slack < n_tiles]).astype(jnp.int32)
    return (dest, te.astype(jnp.int32), nused.reshape(1).astype(jnp.int32),
            zstart.astype(jnp.int32), zflag, n_tiles * tg)


def _moe(hn, h1, p, layer, route, route_t, count, wg, wu, wd, npl, wpg, wpl, tm, tg, tf):
    dest, te, nused, zstart, zflag, n_rows = _routing(route_t, count, hn.shape[0], tm, tg)
    xs = _dispatch(hn, dest, zstart, zflag, n_rows, tm, tg)
    ys = _moe_ffn(xs, te, nused, wg, wu, wd, tg, tf)
    return _combine(ys, dest, route, h1, p, layer, npl, wpg, wpl, tm)


def _alibi_bias():
    slopes = jnp.exp2(-8.0 * jnp.arange(1, N_HEADS + 1, dtype=F32) / N_HEADS)
    qi = jnp.arange(CHUNK, dtype=jnp.int32)[None, :]
    kj = jnp.arange(BAND, dtype=jnp.int32)[:, None]
    dist = jnp.abs(qi + WINDOW - kj).astype(F32)
    bias = slopes[:, None, None] * dist[None]
    bias = bias.reshape(N_KV_HEADS, GROUP, BAND, CHUNK)
    return jnp.transpose(bias, (0, 2, 1, 3)).reshape(N_KV_HEADS, BAND, GROUP * CHUNK)


def _head_sum_matrix():
    head = jnp.arange(QK_WIDTH, dtype=jnp.int32) // HEAD_DIM
    return (head[:, None] == head[None, :]).astype(BF16)


def _layer_params(l, norm_mix, w_in, q_gain, k_gain, attn_sinks, conv_w, conv_b, conv_ln_g, conv_ln_b,
                  out_norm_attn, out_norm_conv, w_out, norm_ffn, router_w, router_b, ple_norm,
                  w_ple_gate, w_ple):
    scale = HEAD_DIM ** -0.5
    prm = {
        "norm_mix": norm_mix[l][None, :],
        "w_in": w_in[l].astype(BF16),
        "qk_gain": jnp.concatenate([jnp.tile(q_gain[l] * scale, N_HEADS), jnp.tile(k_gain[l], N_KV_HEADS)])[None, :],
        "sinks": jnp.broadcast_to(attn_sinks[l].reshape(N_KV_HEADS, 1, GROUP, 1),
                                  (N_KV_HEADS, 1, GROUP, CHUNK)).reshape(N_KV_HEADS, 1, GROUP * CHUNK),
        "conv_w": jnp.pad(conv_w[l], ((0, CONV_PAD - CONV_K), (0, 0))),
        "conv_b": conv_b[l][None, :],
        "cln_g": conv_ln_g[l][None, :],
        "cln_b": conv_ln_b[l][None, :],
        "on_attn": out_norm_attn[l][None, :],
        "on_conv": out_norm_conv[l][None, :],
        "w_out": w_out[l].astype(BF16),
        "norm_ffn": norm_ffn[l][None, :],
        "ple_norm": ple_norm[l][None, :],
        "w_ple_gate": w_ple_gate[l].astype(BF16),
        "w_ple": w_ple[l].astype(BF16),
        "router_w": None,
        "router_b": None,
    }
    if l % 2 == 1:
        i = l // 2
        rw = jnp.pad(router_w[i], ((0, 0), (0, LANES - N_EXPERTS)))
        rw_hi = rw.astype(BF16)
        rw_lo = (rw - rw_hi.astype(F32)).astype(BF16)
        prm["router_w"] = jnp.concatenate([rw_hi, rw_lo], axis=1)
        prm["router_b"] = jnp.pad(router_b[i], (0, LANES - N_EXPERTS))[None, :]
    return prm


def _trunk(x, p, cache_k, cache_v, state_conv, layers, ffn_w, bias, hsum, *, tm, tq):
    nseq, seq, _ = x.shape
    m = nseq * seq
    has_past = cache_k is not None
    h = x.reshape(m, D_MODEL)
    p2 = p.reshape(p.shape[0], m, D_PLE)
    win_k, win_v, convs = [], [], []
    for l, prm in enumerate(layers):
        q, kv, u = _in_proj(h, prm["norm_mix"], prm["w_in"], prm["qk_gain"], hsum, tm)
        if has_past:
            win = cache_k.shape[2]
            kv_prev = jnp.concatenate([cache_k[l].reshape(nseq * win, KV_WIDTH),
                                       cache_v[l].reshape(nseq * win, KV_WIDTH)], axis=1)
            u_prev = jnp.pad(state_conv[l], ((0, 0), (CONV_PAD - (CONV_K - 1), 0), (0, 0))).reshape(
                nseq * CONV_PAD, CONV_CH)
        else:
            kv_prev, u_prev = kv, u
        outs = _mixer(q, kv, kv_prev, u, u_prev, h, dict(prm, bias=bias), nseq=nseq, tq=tq, has_past=has_past)
        h1, hn = outs[0], outs[1]
        wg, wu, wd = ffn_w[l]
        if len(outs) > 2:
            route, route_t, count = outs[2:]
            h = _moe(hn, h1, p2, l, route, route_t, count, wg, wu, wd,
                     prm["ple_norm"], prm["w_ple_gate"], prm["w_ple"], tm, tm, D_FF // 2)
        else:
            h = _ffn(hn, h1, p2, l, wg, wu, wd, prm["ple_norm"], prm["w_ple_gate"], prm["w_ple"], tm, D_FF // 2)
        kv3 = kv.reshape(nseq, seq, 2 * KV_WIDTH)
        u3 = u.reshape(nseq, seq, CONV_CH)
        if has_past:
            kv3 = jnp.concatenate([kv_prev.reshape(nseq, win, 2 * KV_WIDTH), kv3], axis=1)[:, -win:]
            u3 = jnp.concatenate([state_conv[l], u3], axis=1)
        else:
            kv3 = kv3[:, seq - WINDOW:]
        win_k.append(kv3[..., :KV_WIDTH].reshape(nseq, -1, N_KV_HEADS, HEAD_DIM))
        win_v.append(kv3[..., KV_WIDTH:].reshape(nseq, -1, N_KV_HEADS, HEAD_DIM))
        convs.append(u3[:, -(CONV_K - 1):])
    return h.reshape(nseq, seq, D_MODEL), jnp.stack(win_k), jnp.stack(win_v), jnp.stack(convs)


def kernel(x_prompt, x_sample, p_prompt, p_sample, cache_k, cache_v, state_conv, norm_mix, w_in, q_gain, k_gain, attn_sinks, conv_w, conv_b, conv_ln_g, conv_ln_b, out_norm_attn, out_norm_conv, w_out, norm_ffn, ffn_gate, ffn_up, ffn_down, router_w, router_b, moe_gate, moe_up, moe_down, ple_norm, w_ple_gate, w_ple):
    depth = w_in.shape[0]
    layers = [
        _layer_params(l, norm_mix, w_in, q_gain, k_gain, attn_sinks, conv_w, conv_b, conv_ln_g, conv_ln_b,
                      out_norm_attn, out_norm_conv, w_out, norm_ffn, router_w, router_b, ple_norm,
                      w_ple_gate, w_ple)
        for l in range(depth)
    ]
    ffn_w = []
    for l in range(depth):
        i = l // 2
        if l % 2 == 0:
            ffn_w.append((ffn_gate[i].astype(BF16), ffn_up[i].astype(BF16), ffn_down[i].astype(BF16)))
        else:
            ffn_w.append((moe_gate[i].astype(BF16), moe_up[i].astype(BF16), moe_down[i].astype(BF16)))
    bias = _alibi_bias()
    hsum = _head_sum_matrix()
    tm_p = min(512, x_prompt.shape[0] * x_prompt.shape[1])
    tq_p = min(512, x_prompt.shape[1])
    y_p, wk_p, wv_p, cv_p = _trunk(x_prompt, p_prompt, None, None, None, layers, ffn_w, bias, hsum,
                                   tm=tm_p, tq=tq_p)
    tm_s = min(512, x_sample.shape[0] * x_sample.shape[1])
    y_s, wk_s, wv_s, cv_s = _trunk(x_sample, p_sample, cache_k, cache_v, state_conv, layers, ffn_w, bias, hsum,
                                   tm=tm_s, tq=x_sample.shape[1])
    return (y_p, y_s, wk_p, wv_p, cv_p, wk_s, wv_s, cv_s)
```

```python
import functools

import jax
import jax.numpy as jnp
from jax import lax
from jax.experimental import pallas as pl
from jax.experimental.pallas import tpu as pltpu

D_MODEL = 1024
CHUNK = 64
WINDOW = 128
N_HEADS = 8
N_KV_HEADS = 2
HEAD_DIM = 64
GROUP = N_HEADS // N_KV_HEADS
ATTN_WIDTH = N_HEADS * HEAD_DIM
KV_WIDTH = N_KV_HEADS * HEAD_DIM
QK_WIDTH = ATTN_WIDTH + KV_WIDTH
CONV_CH = 512
CONV_K = 31
SUBLANES = 8
CONV_PAD = 32
CONV_ROWS = 64
CONV_ACCS = 2
D_IN = ATTN_WIDTH + 2 * KV_WIDTH + 2 * CONV_CH
BAND = WINDOW + CHUNK
D_FF = 2816
N_EXPERTS = 8
TOP_K = 2
N_FILL = 2 * N_EXPERTS
D_PLE = 256
DMA_ROWS = 8
DISPATCH_TILES = 2
EPS = 1e-6
LANES = 128

F32 = jnp.float32
BF16 = jnp.bfloat16

VMEM_LIMIT = 56 * 1024 * 1024


def _rms(x, g):
    return x * lax.rsqrt(jnp.mean(x * x, axis=-1, keepdims=True) + EPS) * g


def _in_proj_kernel(h_ref, nrm_ref, w_ref, gain_ref, hsum_ref, q_ref, kv_ref, u_ref):
    xn = _rms(h_ref[...], nrm_ref[...])
    z = jnp.dot(xn.astype(BF16), w_ref[...], preferred_element_type=F32)
    qk = z[:, :QK_WIDTH]
    ss = jnp.dot((qk * qk).astype(BF16), hsum_ref[...], preferred_element_type=F32)
    qkn = qk * lax.rsqrt(ss * (1.0 / HEAD_DIM) + EPS) * gain_ref[...]
    q_ref[...] = qkn[:, :ATTN_WIDTH].astype(q_ref.dtype)
    kv_ref[:, :KV_WIDTH] = qkn[:, ATTN_WIDTH:]
    kv_ref[:, KV_WIDTH:] = z[:, QK_WIDTH:QK_WIDTH + KV_WIDTH]
    a = z[:, QK_WIDTH + KV_WIDTH:QK_WIDTH + KV_WIDTH + CONV_CH]
    gl = z[:, QK_WIDTH + KV_WIDTH + CONV_CH:]
    u_ref[...] = a * jax.nn.sigmoid(gl)


def _in_proj(h, nrm, w_in, qk_gain, hsum, tm):
    m = h.shape[0]
    row = lambda i: (i, 0)
    fixed = lambda i: (0, 0)
    return pl.pallas_call(
        _in_proj_kernel,
        grid=(m // tm,),
        in_specs=[
            pl.BlockSpec((tm, D_MODEL), row),
            pl.BlockSpec((1, D_MODEL), fixed),
            pl.BlockSpec((D_MODEL, D_IN), fixed),
            pl.BlockSpec((1, QK_WIDTH), fixed),
            pl.BlockSpec((QK_WIDTH, QK_WIDTH), fixed),
        ],
        out_specs=[
            pl.BlockSpec((tm, ATTN_WIDTH), row),
            pl.BlockSpec((tm, 2 * KV_WIDTH), row),
            pl.BlockSpec((tm, CONV_CH), row),
        ],
        out_shape=[
            jax.ShapeDtypeStruct((m, ATTN_WIDTH), BF16),
            jax.ShapeDtypeStruct((m, 2 * KV_WIDTH), F32),
            jax.ShapeDtypeStruct((m, CONV_CH), F32),
        ],
        compiler_params=pltpu.CompilerParams(
            dimension_semantics=("arbitrary",), vmem_limit_bytes=VMEM_LIMIT),
        name="in_proj",
    )(h, nrm, w_in, qk_gain, hsum)


def _mixer_kernel(q_ref, kv_ref, kvp_ref, u_ref, up_ref, h_ref, bias_ref, sink_ref,
                  cw_ref, cb_ref, lng_ref, lnb_ref, ona_ref, onc_ref, wout_ref, nffn_ref,
                  *rest, tq, has_past, with_router):
    if with_router:
        (rw_ref, rb_ref, ltri_ref, h1_ref, hn_ref, route_ref, routet_ref, count_ref,
         kvx_ref, ux_ref, oa_ref, cv_ref, cnt_ref) = rest
    else:
        h1_ref, hn_ref, kvx_ref, ux_ref, oa_ref, cv_ref = rest
    t = pl.program_id(1)
    nch = tq // CHUNK

    kvx_ref[:WINDOW, :] = kvp_ref[...].astype(BF16)
    kvx_ref[WINDOW:, :] = kv_ref[...].astype(BF16)

    for c in range(nch):
        qc = q_ref[c * CHUNK:(c + 1) * CHUNK, :]
        kvb = kvx_ref[c * CHUNK:c * CHUNK + BAND, :]
        if not has_past and c < WINDOW // CHUNK:
            kpos = lax.broadcasted_iota(jnp.int32, (BAND, 1), 0) + (t * tq + c * CHUNK - WINDOW)
            valid = kpos >= 0
        else:
            valid = None
        outs = []
        for j in range(N_KV_HEADS):
            qs = jnp.concatenate(
                [qc[:, (j * GROUP + g) * HEAD_DIM:(j * GROUP + g + 1) * HEAD_DIM] for g in range(GROUP)],
                axis=0)
            kj = kvb[:, j * HEAD_DIM:(j + 1) * HEAD_DIM]
            vj = kvb[:, KV_WIDTH + j * HEAD_DIM:KV_WIDTH + (j + 1) * HEAD_DIM]
            s = lax.dot_general(kj, qs, (((1,), (1,)), ((), ())), preferred_element_type=F32)
            s = s - bias_ref[j]
            if valid is not None:
                s = jnp.where(valid, s, -jnp.inf)
            sink = sink_ref[j]
            mx = jnp.maximum(jnp.max(s, axis=0, keepdims=True), sink)
            e = jnp.exp(s - mx)
            den = jnp.sum(e, axis=0, keepdims=True) + jnp.exp(sink - mx)
            prob = (e * (1.0 / den)).astype(BF16)
            o = lax.dot_general(prob, vj, (((0,), (0,)), ((), ())), preferred_element_type=F32)
            outs.extend(o[g * CHUNK:(g + 1) * CHUNK, :] for g in range(GROUP))
        oa_ref[c * CHUNK:(c + 1) * CHUNK, :] = jnp.concatenate(outs, axis=1)

    if has_past:
        ux_ref[0, :CONV_PAD, :] = up_ref[...]
    else:
        ux_ref[0, :CONV_PAD, :] = jnp.where(t > 0, up_ref[...], 0.0)
    ux_ref[0, CONV_PAD:, :] = u_ref[...]
    shifted_rows = tq + CONV_PAD - SUBLANES
    for r in range(1, SUBLANES):
        ux_ref[r, :shifted_rows, :] = ux_ref[0, r:r + shifted_rows, :]
    lead = CONV_PAD - (CONV_K - 1)

    for cg in range(CONV_CH // LANES):
        lanes = slice(cg * LANES, (cg + 1) * LANES)
        taps = [cw_ref[k:k + 1, lanes] for k in range(CONV_K)]

        def conv_rows(i, carry, lanes=lanes, taps=taps):
            r0 = pl.multiple_of(i * CONV_ROWS, CONV_ROWS)
            accs = [None] * CONV_ACCS
            for r in range(SUBLANES):
                steps = [(k, (lead + k) // SUBLANES) for k in range(CONV_K) if (lead + k) % SUBLANES == r]
                slab = ux_ref[r, pl.ds(r0, CONV_ROWS + steps[-1][1] * SUBLANES), lanes]
                for k, a in steps:
                    term = slab[a * SUBLANES:a * SUBLANES + CONV_ROWS, :] * taps[k]
                    accs[k % CONV_ACCS] = term if accs[k % CONV_ACCS] is None else accs[k % CONV_ACCS] + term
            cv_ref[pl.ds(r0, CONV_ROWS), lanes] = functools.reduce(lambda x, y: x + y, accs) + cb_ref[:, lanes]
            return carry

        lax.fori_loop(0, tq // CONV_ROWS, conv_rows, 0)

    cv = cv_ref[...]
    mu = jnp.mean(cv, axis=-1, keepdims=True)
    xc = cv - mu
    ln = xc * lax.rsqrt(jnp.mean(xc * xc, axis=-1, keepdims=True) + EPS) * lng_ref[...] + lnb_ref[...]
    oc = ln * jax.nn.sigmoid(ln)
    cn = _rms(oc, onc_ref[...]).astype(BF16)

    an = _rms(oa_ref[...], ona_ref[...]).astype(BF16)
    mixed = (jnp.dot(an, wout_ref[:ATTN_WIDTH, :], preferred_element_type=F32)
             + jnp.dot(cn, wout_ref[ATTN_WIDTH:, :], preferred_element_type=F32))
    h1 = h_ref[...] + mixed
    h1_ref[...] = h1
    hn = _rms(h1, nffn_ref[...])
    hn_ref[...] = hn.astype(hn_ref.dtype)

    if with_router:
        hi = hn.astype(BF16)
        lo = (hn - hi.astype(F32)).astype(BF16)
        part = jnp.dot(hi, rw_ref[...], preferred_element_type=F32)
        logits = (part[:, :LANES] + part[:, LANES:]
                  + jnp.dot(lo, rw_ref[:, :LANES], preferred_element_type=F32) + rb_ref[...])
        col = lax.broadcasted_iota(jnp.int32, logits.shape, 1)
        logits = jnp.where(col < N_EXPERTS, logits, -jnp.inf)
        m1 = jnp.max(logits, axis=-1, keepdims=True)
        i1 = jnp.min(jnp.where(logits == m1, col, LANES), axis=-1, keepdims=True)
        rest_l = jnp.where(col == i1, -jnp.inf, logits)
        m2 = jnp.max(rest_l, axis=-1, keepdims=True)
        i2 = jnp.min(jnp.where(rest_l == m2, col, LANES), axis=-1, keepdims=True)
        e2 = jnp.exp(m2 - m1)
        den = 1.0 + e2

        @pl.when((pl.program_id(0) == 0) & (t == 0))
        def _():
            cnt_ref[...] = jnp.zeros_like(cnt_ref)

        sel = ((col == i1) | (col == i2)).astype(F32)
        ahead = jnp.dot(ltri_ref[...], sel.astype(BF16), preferred_element_type=F32) + cnt_ref[...]
        r1 = jnp.sum(jnp.where(col == i1, ahead, 0.0), axis=-1, keepdims=True)
        r2 = jnp.sum(jnp.where(col == i2, ahead, 0.0), axis=-1, keepdims=True)
        cnt_ref[...] += jnp.sum(sel, axis=0, keepdims=True)
        count_ref[...] = jnp.broadcast_to(cnt_ref[...], count_ref.shape)

        route = jnp.where(
            col == 0, i1.astype(F32),
            jnp.where(col == 1, i2.astype(F32),
                      jnp.where(col == 2, 1.0 / den,
                                jnp.where(col == 3, e2 / den,
                                          jnp.where(col == 4, r1, jnp.where(col == 5, r2, 0.0))))))
        route_ref[...] = route
        routet_ref[0] = route.T[:SUBLANES, :]


def _mixer(q, kv, kv_prev, u, u_prev, h, prm, *, nseq, tq, has_past):
    m = h.shape[0]
    seq = m // nseq
    nt = seq // tq
    with_router = prm["router_w"] is not None
    cur = lambda b, t: (b * nt + t, 0)
    fixed2 = lambda b, t: (0, 0)
    fixed3 = lambda b, t: (0, 0, 0)
    if has_past:
        kvp_map = lambda b, t: (b, 0)
        up_map = lambda b, t: (b, 0)
    else:
        kvp_map = lambda b, t: (jnp.maximum(b * (seq // WINDOW) + t * (tq // WINDOW) - 1, 0), 0)
        up_map = lambda b, t: (jnp.maximum(b * (seq // CONV_PAD) + t * (tq // CONV_PAD) - 1, 0), 0)
    in_specs = [
        pl.BlockSpec((tq, ATTN_WIDTH), cur),
        pl.BlockSpec((tq, 2 * KV_WIDTH), cur),
        pl.BlockSpec((WINDOW, 2 * KV_WIDTH), kvp_map),
        pl.BlockSpec((tq, CONV_CH), cur),
        pl.BlockSpec((CONV_PAD, CONV_CH), up_map),
        pl.BlockSpec((tq, D_MODEL), cur),
        pl.BlockSpec((N_KV_HEADS, BAND, GROUP * CHUNK), fixed3),
        pl.BlockSpec((N_KV_HEADS, 1, GROUP * CHUNK), fixed3),
        pl.BlockSpec((CONV_PAD, CONV_CH), fixed2),
        pl.BlockSpec((1, CONV_CH), fixed2),
        pl.BlockSpec((1, CONV_CH), fixed2),
        pl.BlockSpec((1, CONV_CH), fixed2),
        pl.BlockSpec((1, ATTN_WIDTH), fixed2),
        pl.BlockSpec((1, CONV_CH), fixed2),
        pl.BlockSpec((D_MODEL, D_MODEL), fixed2),
        pl.BlockSpec((1, D_MODEL), fixed2),
    ]
    args = [q, kv, kv_prev, u, u_prev, h, prm["bias"], prm["sinks"], prm["conv_w"], prm["conv_b"],
            prm["cln_g"], prm["cln_b"], prm["on_attn"], prm["on_conv"], prm["w_out"], prm["norm_ffn"]]
    out_specs = [pl.BlockSpec((tq, D_MODEL), cur), pl.BlockSpec((tq, D_MODEL), cur)]
    out_shape = [jax.ShapeDtypeStruct((m, D_MODEL), F32),
                 jax.ShapeDtypeStruct((m, D_MODEL), F32 if with_router else BF16)]
    scratch = [
        pltpu.VMEM((WINDOW + tq, 2 * KV_WIDTH), BF16),
        pltpu.VMEM((SUBLANES, CONV_PAD + tq, CONV_CH), F32),
        pltpu.VMEM((tq, ATTN_WIDTH), F32),
        pltpu.VMEM((tq, CONV_CH), F32),
    ]
    if with_router:
        ltri = (jnp.arange(tq, dtype=jnp.int32)[:, None] > jnp.arange(tq, dtype=jnp.int32)[None, :]).astype(BF16)
        in_specs += [pl.BlockSpec((D_MODEL, 2 * LANES), fixed2), pl.BlockSpec((1, LANES), fixed2),
                     pl.BlockSpec((tq, tq), fixed2)]
        args += [prm["router_w"], prm["router_b"], ltri]
        out_specs += [pl.BlockSpec((tq, LANES), cur),
                      pl.BlockSpec((1, SUBLANES, tq), lambda b, t: (b * nt + t, 0, 0)),
                      pl.BlockSpec((SUBLANES, LANES), fixed2)]
        out_shape += [jax.ShapeDtypeStruct((m, LANES), F32),
                      jax.ShapeDtypeStruct((nseq * nt, SUBLANES, tq), F32),
                      jax.ShapeDtypeStruct((SUBLANES, LANES), F32)]
        scratch.append(pltpu.VMEM((1, LANES), F32))
    return pl.pallas_call(
        functools.partial(_mixer_kernel, tq=tq, has_past=has_past, with_router=with_router),
        grid=(nseq, nt),
        in_specs=in_specs,
        out_specs=out_specs,
        out_shape=out_shape,
        scratch_shapes=scratch,
        compiler_params=pltpu.CompilerParams(
            dimension_semantics=("arbitrary", "arbitrary"), vmem_limit_bytes=VMEM_LIMIT),
        name="mixer_past" if has_past else "mixer",
    )(*args)


def _ple(h2, p_ref, npl_ref, wpg_ref, wpl_ref):
    r = _rms(h2, npl_ref[...]).astype(BF16)
    gate = jax.nn.sigmoid(jnp.dot(r, wpg_ref[...], preferred_element_type=F32))
    pe = jnp.dot(p_ref[...].astype(BF16), wpl_ref[...], preferred_element_type=F32)
    return h2 + gate * pe


def _swiglu(x, wg, wu, wd):
    g = jnp.dot(x, wg, preferred_element_type=F32)
    u = jnp.dot(x, wu, preferred_element_type=F32)
    a = (g * jax.nn.sigmoid(g) * u).astype(BF16)
    return jnp.dot(a, wd, preferred_element_type=F32)


def _ffn_kernel(x_ref, h_ref, p_ref, wg_ref, wu_ref, wd_ref, npl_ref, wpg_ref, wpl_ref, o_ref):
    f = pl.program_id(1)

    @pl.when(f == 0)
    def _():
        o_ref[...] = h_ref[...]

    o_ref[...] += _swiglu(x_ref[...], wg_ref[...], wu_ref[...], wd_ref[...])

    @pl.when(f == pl.num_programs(1) - 1)
    def _():
        o_ref[...] = _ple(o_ref[...], p_ref, npl_ref, wpg_ref, wpl_ref)


def _ffn(x, h, p, layer, wg, wu, wd, npl, wpg, wpl, tm, tf):
    m = x.shape[0]
    row = lambda i, f: (i, 0)
    fixed = lambda i, f: (0, 0)
    return pl.pallas_call(
        _ffn_kernel,
        grid=(m // tm, D_FF // tf),
        in_specs=[
            pl.BlockSpec((tm, D_MODEL), row),
            pl.BlockSpec((tm, D_MODEL), row),
            pl.BlockSpec((None, tm, D_PLE), lambda i, f: (layer, i, 0)),
            pl.BlockSpec((D_MODEL, tf), lambda i, f: (0, f)),
            pl.BlockSpec((D_MODEL, tf), lambda i, f: (0, f)),
            pl.BlockSpec((tf, D_MODEL), lambda i, f: (f, 0)),
            pl.BlockSpec((1, D_MODEL), fixed),
            pl.BlockSpec((D_MODEL, D_MODEL), fixed),
            pl.BlockSpec((D_PLE, D_MODEL), fixed),
        ],
        out_specs=pl.BlockSpec((tm, D_MODEL), row),
        out_shape=jax.ShapeDtypeStruct((m, D_MODEL), F32),
        compiler_params=pltpu.CompilerParams(
            dimension_semantics=("arbitrary", "arbitrary"), vmem_limit_bytes=VMEM_LIMIT),
        name="ffn",
    )(x, h, p, wg, wu, wd, npl, wpg, wpl)


def _row_copy(src_ref, src_row, dst_ref, dst_row, sem):
    return pltpu.make_async_copy(src_ref.at[pl.ds(src_row, 1)], dst_ref.at[pl.ds(dst_row, 1)], sem)


def _dispatch_kernel(zstart_ref, zflag_ref, dest_ref, x_ref, xs_ref, zero_ref, zsem, sem, *, tm, tg, tiles):
    i = pl.program_id(0)

    @pl.when(i == 0)
    def _():
        zero_ref[...] = jnp.zeros_like(zero_ref)
        def fill(e):
            return pltpu.make_async_copy(zero_ref, xs_ref.at[pl.ds(pl.multiple_of(zstart_ref[e], tg), tg)], zsem)

        for e in range(N_FILL):
            @pl.when(zflag_ref[e] > 0)
            def _():
                fill(e).start()
        for e in range(N_FILL):
            @pl.when(zflag_ref[e] > 0)
            def _():
                fill(e).wait()

    for k in range(tiles):
        def issue(g, carry, k=k):
            base = pl.multiple_of(g * DMA_ROWS, DMA_ROWS)
            rows = x_ref.at[pl.ds(k * tm + base, DMA_ROWS)]
            for j in range(DMA_ROWS):
                for s in range(TOP_K):
                    _row_copy(rows, j, xs_ref, dest_ref[k, 0, s * tm + base + j], sem).start()
            return carry

        lax.fori_loop(0, tm // DMA_ROWS, issue, 0)

    for s in range(TOP_K):
        pltpu.make_async_copy(x_ref, xs_ref.at[pl.ds(0, tiles * tm)], sem).wait()


def _dispatch(x, dest, zstart, zflag, n_rows, tm, tg):
    m = x.shape[0]
    tiles = DISPATCH_TILES if (m // tm) % DISPATCH_TILES == 0 else 1
    return pl.pallas_call(
        functools.partial(_dispatch_kernel, tm=tm, tg=tg, tiles=tiles),
        grid_spec=pltpu.PrefetchScalarGridSpec(
            num_scalar_prefetch=2,
            grid=(m // (tiles * tm),),
            in_specs=[
                pl.BlockSpec((tiles, 1, TOP_K * tm), lambda i, zs, zf: (i, 0, 0), memory_space=pltpu.SMEM),
                pl.BlockSpec((tiles * tm, D_MODEL), lambda i, zs, zf: (i, 0)),
            ],
            out_specs=pl.BlockSpec(memory_space=pl.ANY),
            scratch_shapes=[
                pltpu.VMEM((tg, D_MODEL), F32),
                pltpu.SemaphoreType.DMA(()),
                pltpu.SemaphoreType.DMA(()),
            ],
        ),
        out_shape=jax.ShapeDtypeStruct((n_rows, D_MODEL), F32),
        compiler_params=pltpu.CompilerParams(
            dimension_semantics=("arbitrary",), vmem_limit_bytes=VMEM_LIMIT),
        name="moe_dispatch",
    )(zstart, zflag, dest, x)


def _moe_ffn_kernel(te_ref, nused_ref, x_ref, wg_ref, wu_ref, wd_ref, o_ref):
    i = pl.program_id(0)
    f = pl.program_id(1)

    @pl.when((i >= nused_ref[0]) & (f == 0))
    def _():
        o_ref[...] = jnp.zeros_like(o_ref)

    @pl.when(i < nused_ref[0])
    def _():
        y = _swiglu(x_ref[...].astype(BF16), wg_ref[0], wu_ref[0], wd_ref[0])

        @pl.when(f == 0)
        def _():
            o_ref[...] = y

        @pl.when(f > 0)
        def _():
            o_ref[...] += y


def _moe_ffn(xs, te, nused, wg, wu, wd, tg, tf):
    n_rows = xs.shape[0]
    nf = D_FF // tf
    tile = lambda i, nu: jnp.minimum(i, nu[0] - 1)
    fstep = lambda i, f, nu: jnp.where(i < nu[0], f, nf - 1)
    return pl.pallas_call(
        _moe_ffn_kernel,
        grid_spec=pltpu.PrefetchScalarGridSpec(
            num_scalar_prefetch=2,
            grid=(n_rows // tg, nf),
            in_specs=[
                pl.BlockSpec((tg, D_MODEL), lambda i, f, te, nu: (tile(i, nu), 0)),
                pl.BlockSpec((1, D_MODEL, tf), lambda i, f, te, nu: (te[tile(i, nu)], 0, fstep(i, f, nu))),
                pl.BlockSpec((1, D_MODEL, tf), lambda i, f, te, nu: (te[tile(i, nu)], 0, fstep(i, f, nu))),
                pl.BlockSpec((1, tf, D_MODEL), lambda i, f, te, nu: (te[tile(i, nu)], fstep(i, f, nu), 0)),
            ],
            out_specs=pl.BlockSpec((tg, D_MODEL), lambda i, f, te, nu: (i, 0)),
        ),
        out_shape=jax.ShapeDtypeStruct((n_rows, D_MODEL), F32),
        compiler_params=pltpu.CompilerParams(
            dimension_semantics=("arbitrary", "arbitrary"), vmem_limit_bytes=VMEM_LIMIT),
        name="moe_ffn",
    )(te, nused, xs, wg, wu, wd)


def _combine_kernel(dest_ref, dest_next_ref, route_ref, h_ref, p_ref, npl_ref, wpg_ref, wpl_ref, ys_ref, o_ref,
                    buf_ref, sem, *, tm):
    i = pl.program_id(0)
    cur = i % 2

    def gather(rows_ref, half):
        def issue(g, carry):
            base = pl.multiple_of(g * DMA_ROWS, DMA_ROWS)
            for s in range(TOP_K):
                rows = buf_ref.at[half, s, pl.ds(base, DMA_ROWS)]
                for j in range(DMA_ROWS):
                    _row_copy(ys_ref, rows_ref[0, 0, s * tm + base + j], rows, j, sem.at[half]).start()
            return carry

        lax.fori_loop(0, tm // DMA_ROWS, issue, 0)

    @pl.when(i == 0)
    def _():
        gather(dest_ref, 0)

    @pl.when(i + 1 < pl.num_programs(0))
    def _():
        gather(dest_next_ref, 1 - cur)

    for s in range(TOP_K):
        pltpu.make_async_copy(ys_ref.at[pl.ds(0, tm)], buf_ref.at[cur, s], sem.at[cur]).wait()

    g1 = route_ref[:, 2:3]
    g2 = route_ref[:, 3:4]
    h2 = h_ref[...] + (g1 * buf_ref[cur, 0] + g2 * buf_ref[cur, 1])
    o_ref[...] = _ple(h2, p_ref, npl_ref, wpg_ref, wpl_ref)


def _combine(ys, dest, route, h, p, layer, npl, wpg, wpl, tm):
    m = h.shape[0]
    row = lambda i: (i, 0)
    fixed = lambda i: (0, 0)
    return pl.pallas_call(
        functools.partial(_combine_kernel, tm=tm),
        grid=(m // tm,),
        in_specs=[
            pl.BlockSpec((1, 1, TOP_K * tm), lambda i: (i, 0, 0), memory_space=pltpu.SMEM),
            pl.BlockSpec((1, 1, TOP_K * tm), lambda i: (jnp.minimum(i + 1, m // tm - 1), 0, 0),
                         memory_space=pltpu.SMEM),
            pl.BlockSpec((tm, LANES), row),
            pl.BlockSpec((tm, D_MODEL), row),
            pl.BlockSpec((None, tm, D_PLE), lambda i: (layer, i, 0)),
            pl.BlockSpec((1, D_MODEL), fixed),
            pl.BlockSpec((D_MODEL, D_MODEL), fixed),
            pl.BlockSpec((D_PLE, D_MODEL), fixed),
            pl.BlockSpec(memory_space=pl.ANY),
        ],
        out_specs=pl.BlockSpec((tm, D_MODEL), row),
        out_shape=jax.ShapeDtypeStruct((m, D_MODEL), F32),
        scratch_shapes=[
            pltpu.VMEM((2, TOP_K, tm, D_MODEL), F32),
            pltpu.SemaphoreType.DMA((2,)),
        ],
        compiler_params=pltpu.CompilerParams(
            dimension_semantics=("arbitrary",), vmem_limit_bytes=VMEM_LIMIT),
        name="moe_combine",
    )(dest, dest, route, h, p, npl, wpg, wpl, ys)


def _routing(route_t, count, m, tm, tg):
    rows = jnp.transpose(route_t, (1, 0, 2)).reshape(SUBLANES, m)
    counts = count[0, :N_EXPERTS].astype(jnp.int32)
    padded = (counts + tg - 1) // tg * tg
    ends = jnp.cumsum(padded)
    offs = ends - padded

    def dest_rows(expert, rank):
        expert = expert.astype(jnp.int32)
        start = sum(jnp.where(expert == k, offs[k], 0) for k in range(N_EXPERTS))
        return (start + rank.astype(jnp.int32)).reshape(m // tm, 1, tm)

    dest = jnp.concatenate([dest_rows(rows[s], rows[2 * TOP_K + s]) for s in range(TOP_K)], axis=2)
    n_tiles = TOP_K * m // tg + N_EXPERTS
    nused = ends[-1] // tg
    tile_ids = jnp.minimum(jnp.arange(n_tiles, dtype=jnp.int32), nused - 1)
    te = jnp.sum((tile_ids[:, None] >= (ends // tg)[None, :]).astype(jnp.int32), axis=1)
    te = jnp.minimum(te, N_EXPERTS - 1)
    slack = nused + jnp.arange(N_FILL - N_EXPERTS, dtype=jnp.int32)
    zstart = jnp.concatenate([jnp.maximum(ends - tg, 0), jnp.minimum(slack, n_tiles - 1) * tg])
    zflag = jnp.concatenate([counts > 0, slack < n_tiles]).astype(jnp.int32)
    return (dest, te.astype(jnp.int32), nused.reshape(1).astype(jnp.int32),
            zstart.astype(jnp.int32), zflag, n_tiles * tg)


def _moe(hn, h1, p, layer, route, route_t, count, wg, wu, wd, npl, wpg, wpl, tm, tg, tf):
    dest, te, nused, zstart, zflag, n_rows = _routing(route_t, count, hn.shape[0], tm, tg)
    xs = _dispatch(hn, dest, zstart, zflag, n_rows, tm, tg)
    ys = _moe_ffn(xs, te, nused, wg, wu, wd, tg, tf)
    return _combine(ys, dest, route, h1, p, layer, npl, wpg, wpl, tm)


def _alibi_bias():
    slopes = jnp.exp2(-8.0 * jnp.arange(1, N_HEADS + 1, dtype=F32) / N_HEADS)
    qi = jnp.arange(CHUNK, dtype=jnp.int32)[None, :]
    kj = jnp.arange(BAND, dtype=jnp.int32)[:, None]
    dist = jnp.abs(qi + WINDOW - kj).astype(F32)
    bias = slopes[:, None, None] * dist[None]
    bias = bias.reshape(N_KV_HEADS, GROUP, BAND, CHUNK)
    return jnp.transpose(bias, (0, 2, 1, 3)).reshape(N_KV_HEADS, BAND, GROUP * CHUNK)


def _head_sum_matrix():
    head = jnp.arange(QK_WIDTH, dtype=jnp.int32) // HEAD_DIM
    return (head[:, None] == head[None, :]).astype(BF16)


def _layer_params(l, norm_mix, w_in, q_gain, k_gain, attn_sinks, conv_w, conv_b, conv_ln_g, conv_ln_b,
                  out_norm_attn, out_norm_conv, w_out, norm_ffn, router_w, router_b, ple_norm,
                  w_ple_gate, w_ple):
    scale = HEAD_DIM ** -0.5
    prm = {
        "norm_mix": norm_mix[l][None, :],
        "w_in": w_in[l].astype(BF16),
        "qk_gain": jnp.concatenate([jnp.tile(q_gain[l] * scale, N_HEADS), jnp.tile(k_gain[l], N_KV_HEADS)])[None, :],
        "sinks": jnp.broadcast_to(attn_sinks[l].reshape(N_KV_HEADS, 1, GROUP, 1),
                                  (N_KV_HEADS, 1, GROUP, CHUNK)).reshape(N_KV_HEADS, 1, GROUP * CHUNK),
        "conv_w": jnp.pad(conv_w[l], ((0, CONV_PAD - CONV_K), (0, 0))),
        "conv_b": conv_b[l][None, :],
        "cln_g": conv_ln_g[l][None, :],
        "cln_b": conv_ln_b[l][None, :],
        "on_attn": out_norm_attn[l][None, :],
        "on_conv": out_norm_conv[l][None, :],
        "w_out": w_out[l].astype(BF16),
        "norm_ffn": norm_ffn[l][None, :],
        "ple_norm": ple_norm[l][None, :],
        "w_ple_gate": w_ple_gate[l].astype(BF16),
        "w_ple": w_ple[l].astype(BF16),
        "router_w": None,
        "router_b": None,
    }
    if l % 2 == 1:
        i = l // 2
        rw = jnp.pad(router_w[i], ((0, 0), (0, LANES - N_EXPERTS)))
        rw_hi = rw.astype(BF16)
        rw_lo = (rw - rw_hi.astype(F32)).astype(BF16)
        prm["router_w"] = jnp.concatenate([rw_hi, rw_lo], axis=1)
        prm["router_b"] = jnp.pad(router_b[i], (0, LANES - N_EXPERTS))[None, :]
    return prm


def _trunk(x, p, cache_k, cache_v, state_conv, layers, ffn_w, bias, hsum, *, tm, tq):
    nseq, seq, _ = x.shape
    m = nseq * seq
    has_past = cache_k is not None
    h = x.reshape(m, D_MODEL)
    p2 = p.reshape(p.shape[0], m, D_PLE)
    win_k, win_v, convs = [], [], []
    for l, prm in enumerate(layers):
        q, kv, u = _in_proj(h, prm["norm_mix"], prm["w_in"], prm["qk_gain"], hsum, tm)
        if has_past:
            win = cache_k.shape[2]
            kv_prev = jnp.concatenate([cache_k[l].reshape(nseq * win, KV_WIDTH),
                                       cache_v[l].reshape(nseq * win, KV_WIDTH)], axis=1)
            u_prev = jnp.pad(state_conv[l], ((0, 0), (CONV_PAD - (CONV_K - 1), 0), (0, 0))).reshape(
                nseq * CONV_PAD, CONV_CH)
        else:
            kv_prev, u_prev = kv, u
        outs = _mixer(q, kv, kv_prev, u, u_prev, h, dict(prm, bias=bias), nseq=nseq, tq=tq, has_past=has_past)
        h1, hn = outs[0], outs[1]
        wg, wu, wd = ffn_w[l]
        if len(outs) > 2:
            route, route_t, count = outs[2:]
            h = _moe(hn, h1, p2, l, route, route_t, count, wg, wu, wd,
                     prm["ple_norm"], prm["w_ple_gate"], prm["w_ple"], tm, tm, D_FF // 2)
        else:
            h = _ffn(hn, h1, p2, l, wg, wu, wd, prm["ple_norm"], prm["w_ple_gate"], prm["w_ple"], tm, D_FF // 2)
        kv3 = kv.reshape(nseq, seq, 2 * KV_WIDTH)
        u3 = u.reshape(nseq, seq, CONV_CH)
        if has_past:
            kv3 = jnp.concatenate([kv_prev.reshape(nseq, win, 2 * KV_WIDTH), kv3], axis=1)[:, -win:]
            u3 = jnp.concatenate([state_conv[l], u3], axis=1)
        else:
            kv3 = kv3[:, seq - WINDOW:]
        win_k.append(kv3[..., :KV_WIDTH].reshape(nseq, -1, N_KV_HEADS, HEAD_DIM))
        win_v.append(kv3[..., KV_WIDTH:].reshape(nseq, -1, N_KV_HEADS, HEAD_DIM))
        convs.append(u3[:, -(CONV_K - 1):])
    return h.reshape(nseq, seq, D_MODEL), jnp.stack(win_k), jnp.stack(win_v), jnp.stack(convs)


def kernel(x_prompt, x_sample, p_prompt, p_sample, cache_k, cache_v, state_conv, norm_mix, w_in, q_gain, k_gain, attn_sinks, conv_w, conv_b, conv_ln_g, conv_ln_b, out_norm_attn, out_norm_conv, w_out, norm_ffn, ffn_gate, ffn_up, ffn_down, router_w, router_b, moe_gate, moe_up, moe_down, ple_norm, w_ple_gate, w_ple):
    depth = w_in.shape[0]
    layers = [
        _layer_params(l, norm_mix, w_in, q_gain, k_gain, attn_sinks, conv_w, conv_b, conv_ln_g, conv_ln_b,
                      out_norm_attn, out_norm_conv, w_out, norm_ffn, router_w, router_b, ple_norm,
                      w_ple_gate, w_ple)
        for l in range(depth)
    ]
    ffn_w = []
    for l in range(depth):
        i = l // 2
        if l % 2 == 0:
            ffn_w.append((ffn_gate[i].astype(BF16), ffn_up[i].astype(BF16), ffn_down[i].astype(BF16)))
        else:
            ffn_w.append((moe_gate[i].astype(BF16), moe_up[i].astype(BF16), moe_down[i].astype(BF16)))
    bias = _alibi_bias()
    hsum = _head_sum_matrix()
    tm_p = min(512, x_prompt.shape[0] * x_prompt.shape[1])
    tq_p = min(512, x_prompt.shape[1])
    y_p, wk_p, wv_p, cv_p = _trunk(x_prompt, p_prompt, None, None, None, layers, ffn_w, bias, hsum,
                                   tm=tm_p, tq=tq_p)
    tm_s = min(512, x_sample.shape[0] * x_sample.shape[1])
    y_s, wk_s, wv_s, cv_s = _trunk(x_sample, p_sample, cache_k, cache_v, state_conv, layers, ffn_w, bias, hsum,
                                   tm=tm_s, tq=x_sample.shape[1])
    return (y_p, y_s, wk_p, wv_p, cv_p, wk_s, wv_s, cv_s)
```

```python
import functools

import jax
import jax.numpy as jnp
from jax import lax
from jax.experimental import pallas as pl
from jax.experimental.pallas import tpu as pltpu

D_MODEL = 1024
CHUNK = 64
WINDOW = 128
N_HEADS = 8
N_KV_HEADS = 2
HEAD_DIM = 64
GROUP = N_HEADS // N_KV_HEADS
HEAD_PACK = 4
ATTN_WIDTH = N_HEADS * HEAD_DIM
KV_WIDTH = N_KV_HEADS * HEAD_DIM
QK_WIDTH = ATTN_WIDTH + KV_WIDTH
CONV_CH = 512
CONV_K = 31
SUBLANES = 8
CONV_PAD = 32
CONV_ROWS = 64
CONV_ACCS = 2
D_IN = ATTN_WIDTH + 2 * KV_WIDTH + 2 * CONV_CH
BAND = WINDOW + CHUNK
D_FF = 2816
N_EXPERTS = 8
TOP_K = 2
N_FILL = 2 * N_EXPERTS
D_PLE = 256
DMA_ROWS = 8
DISPATCH_TILES = 2
EPS = 1e-6
LANES = 128

F32 = jnp.float32
BF16 = jnp.bfloat16

VMEM_LIMIT = 56 * 1024 * 1024

def _rms(x, g):
    return x * lax.rsqrt(jnp.mean(x * x, axis=-1, keepdims=True) + EPS) * g


def _in_proj_kernel(h_ref, nrm_ref, w_ref, gain_ref, hsum_ref, q_ref, kv_ref, u_ref):
    xn = _rms(h_ref[...], nrm_ref[...])
    z = jnp.dot(xn.astype(BF16), w_ref[...], preferred_element_type=F32)
    qk = z[:, :QK_WIDTH]
    ss = jnp.dot((qk * qk).astype(BF16), hsum_ref[...], preferred_element_type=F32)
    qkn = qk * lax.rsqrt(ss * (1.0 / HEAD_DIM) + EPS) * gain_ref[...]
    q_ref[...] = qkn[:, :ATTN_WIDTH].astype(q_ref.dtype)
    kv_ref[:, :KV_WIDTH] = qkn[:, ATTN_WIDTH:]
    kv_ref[:, KV_WIDTH:] = z[:, QK_WIDTH:QK_WIDTH + KV_WIDTH]
    a = z[:, QK_WIDTH + KV_WIDTH:QK_WIDTH + KV_WIDTH + CONV_CH]
    gl = z[:, QK_WIDTH + KV_WIDTH + CONV_CH:]
    u_ref[...] = a * jax.nn.sigmoid(gl)


def _in_proj(h, nrm, w_in, qk_gain, hsum, tm):
    m = h.shape[0]
    row = lambda i: (i, 0)
    fixed = lambda i: (0, 0)
    return pl.pallas_call(
        _in_proj_kernel,
        grid=(m // tm,),
        in_specs=[
            pl.BlockSpec((tm, D_MODEL), row),
            pl.BlockSpec((1, D_MODEL), fixed),
            pl.BlockSpec((D_MODEL, D_IN), fixed),
            pl.BlockSpec((1, QK_WIDTH), fixed),
            pl.BlockSpec((QK_WIDTH, QK_WIDTH), fixed),
        ],
        out_specs=[
            pl.BlockSpec((tm, ATTN_WIDTH), row),
            pl.BlockSpec((tm, 2 * KV_WIDTH), row),
            pl.BlockSpec((tm, CONV_CH), row),
        ],
        out_shape=[
            jax.ShapeDtypeStruct((m, ATTN_WIDTH), BF16),
            jax.ShapeDtypeStruct((m, 2 * KV_WIDTH), F32),
            jax.ShapeDtypeStruct((m, CONV_CH), F32),
        ],
        compiler_params=pltpu.CompilerParams(
            dimension_semantics=("arbitrary",), vmem_limit_bytes=VMEM_LIMIT),
        name="in_proj",
    )(h, nrm, w_in, qk_gain, hsum)


def _mixer_kernel(q_ref, kv_ref, kvp_ref, u_ref, up_ref, h_ref, bias_ref, sink_ref,
                  cw_ref, cb_ref, lng_ref, lnb_ref, ona_ref, onc_ref, wout_ref, nffn_ref,
                  *rest, tq, has_past, with_router):
    if with_router:
        (rw_ref, rb_ref, utri_ref, h1_ref, hn_ref, route_ref, routet_ref, count_ref,
         kvx_ref, ux_ref, oa_ref, cv_ref, s_ref, p_ref, cnt_ref) = rest
    else:
        h1_ref, hn_ref, kvx_ref, ux_ref, oa_ref, cv_ref, s_ref, p_ref = rest
    t = pl.program_id(1)
    nch = tq // CHUNK

    kvx_ref[:WINDOW, :] = kvp_ref[...].astype(BF16)
    kvx_ref[WINDOW:, :] = kv_ref[...].astype(BF16)

    packs = N_HEADS // HEAD_PACK
    units = [(c, hp) for c in range(nch) for hp in range(packs)]

    def unit_operands(hp):
        j = hp * HEAD_PACK // GROUP
        lanes = slice((hp * HEAD_PACK % GROUP) * CHUNK, (hp * HEAD_PACK % GROUP + HEAD_PACK) * CHUNK)
        return j, lanes

    for u, (c, hp) in enumerate(units):
        j, lanes = unit_operands(hp)
        qc = q_ref[c * CHUNK:(c + 1) * CHUNK, :]
        qs = jnp.concatenate([qc[:, h * HEAD_DIM:(h + 1) * HEAD_DIM]
                              for h in range(hp * HEAD_PACK, (hp + 1) * HEAD_PACK)],
                             axis=0)
        kj = kvx_ref[c * CHUNK:c * CHUNK + BAND, j * HEAD_DIM:(j + 1) * HEAD_DIM]
        s = lax.dot_general(kj, qs, (((1,), (1,)), ((), ())), preferred_element_type=F32)
        s = s - bias_ref[j, :, lanes]
        if not has_past and c < WINDOW // CHUNK:
            kpos = lax.broadcasted_iota(jnp.int32, (BAND, 1), 0) + (t * tq + c * CHUNK - WINDOW)
            s = jnp.where(kpos >= 0, s, -jnp.inf)
        s_ref[u] = s

    for u, (c, hp) in enumerate(units):
        j, lanes = unit_operands(hp)
        s = s_ref[u]
        sink = sink_ref[j, :, lanes]
        mx = jnp.maximum(jnp.max(s, axis=0, keepdims=True), sink)
        e = jnp.exp(s - mx)
        den = jnp.sum(e, axis=0, keepdims=True) + jnp.exp(sink - mx)
        p_ref[u] = (e * (1.0 / den)).astype(BF16)

    for c in range(nch):
        outs = []
        for hp in range(packs):
            j, _ = unit_operands(hp)
            vj = kvx_ref[c * CHUNK:c * CHUNK + BAND, KV_WIDTH + j * HEAD_DIM:KV_WIDTH + (j + 1) * HEAD_DIM]
            o = lax.dot_general(p_ref[c * packs + hp], vj, (((0,), (0,)), ((), ())),
                                preferred_element_type=F32)
            outs.extend(o[g * CHUNK:(g + 1) * CHUNK, :] for g in range(HEAD_PACK))
        oa_ref[c * CHUNK:(c + 1) * CHUNK, :] = jnp.concatenate(outs, axis=1)

    if has_past:
        ux_ref[0, :CONV_PAD, :] = up_ref[...]
    else:
        ux_ref[0, :CONV_PAD, :] = jnp.where(t > 0, up_ref[...], 0.0)
    ux_ref[0, CONV_PAD:, :] = u_ref[...]
    shifted_rows = tq + CONV_PAD - SUBLANES
    for r in range(1, SUBLANES):
        ux_ref[r, :shifted_rows, :] = ux_ref[0, r:r + shifted_rows, :]
    lead = CONV_PAD - (CONV_K - 1)

    for cg in range(CONV_CH // LANES):
        lanes = slice(cg * LANES, (cg + 1) * LANES)
        taps = [cw_ref[k:k + 1, lanes] for k in range(CONV_K)]

        def conv_rows(i, carry, lanes=lanes, taps=taps):
            r0 = pl.multiple_of(i * CONV_ROWS, CONV_ROWS)
            accs = [None] * CONV_ACCS
            for r in range(SUBLANES):
                steps = [(k, (lead + k) // SUBLANES) for k in range(CONV_K) if (lead + k) % SUBLANES == r]
                slab = ux_ref[r, pl.ds(r0, CONV_ROWS + steps[-1][1] * SUBLANES), lanes]
                for k, a in steps:
                    term = slab[a * SUBLANES:a * SUBLANES + CONV_ROWS, :] * taps[k]
                    accs[k % CONV_ACCS] = term if accs[k % CONV_ACCS] is None else accs[k % CONV_ACCS] + term
            cv_ref[pl.ds(r0, CONV_ROWS), lanes] = functools.reduce(lambda x, y: x + y, accs) + cb_ref[:, lanes]
            return carry

        lax.fori_loop(0, tq // CONV_ROWS, conv_rows, 0)

    cv = cv_ref[...]
    mu = jnp.mean(cv, axis=-1, keepdims=True)
    xc = cv - mu
    ln = xc * lax.rsqrt(jnp.mean(xc * xc, axis=-1, keepdims=True) + EPS) * lng_ref[...] + lnb_ref[...]
    oc = ln * jax.nn.sigmoid(ln)
    cn = _rms(oc, onc_ref[...]).astype(BF16)

    an = _rms(oa_ref[...], ona_ref[...]).astype(BF16)
    mixed = (jnp.dot(an, wout_ref[:ATTN_WIDTH, :], preferred_element_type=F32)
             + jnp.dot(cn, wout_ref[ATTN_WIDTH:, :], preferred_element_type=F32))
    h1 = h_ref[...] + mixed
    h1_ref[...] = h1
    hn = _rms(h1, nffn_ref[...])
    hn_ref[...] = hn.astype(hn_ref.dtype)

    if with_router:
        hi = hn.astype(BF16)
        lo = (hn - hi.astype(F32)).astype(BF16)
        part = jnp.dot(hi, rw_ref[...], preferred_element_type=F32)
        logits = (part[:, :LANES] + part[:, LANES:]
                  + jnp.dot(lo, rw_ref[:, :LANES], preferred_element_type=F32) + rb_ref[...])
        lt = logits.T[:N_EXPERTS, :]
        row = lax.broadcasted_iota(jnp.int32, lt.shape, 0)
        m1 = jnp.max(lt, axis=0, keepdims=True)
        i1 = jnp.min(jnp.where(lt == m1, row, N_EXPERTS), axis=0, keepdims=True)
        rest_l = jnp.where(row == i1, -jnp.inf, lt)
        m2 = jnp.max(rest_l, axis=0, keepdims=True)
        i2 = jnp.min(jnp.where(rest_l == m2, row, N_EXPERTS), axis=0, keepdims=True)
        e2 = jnp.exp(m2 - m1)
        den = 1.0 + e2

        @pl.when((pl.program_id(0) == 0) & (t == 0))
        def _():
            cnt_ref[...] = jnp.zeros_like(cnt_ref)

        sel = ((row == i1) | (row == i2)).astype(F32)
        ahead = jnp.dot(sel.astype(BF16), utri_ref[...], preferred_element_type=F32) + cnt_ref[:, 0:1]
        r1 = jnp.sum(jnp.where(row == i1, ahead, 0.0), axis=0, keepdims=True)
        r2 = jnp.sum(jnp.where(row == i2, ahead, 0.0), axis=0, keepdims=True)
        cnt_ref[...] += jnp.sum(sel, axis=1, keepdims=True)
        count_ref[...] = cnt_ref[...]

        route_t = jnp.concatenate(
            [i1.astype(F32), i2.astype(F32), 1.0 / den, e2 / den, r1, r2,
             jnp.zeros((SUBLANES - 3 * TOP_K, tq), F32)], axis=0)
        routet_ref[0] = route_t
        route_ref[...] = jnp.concatenate([route_t, jnp.zeros((LANES - SUBLANES, tq), F32)], axis=0).T


def _mixer(q, kv, kv_prev, u, u_prev, h, prm, *, nseq, tq, has_past):
    m = h.shape[0]
    seq = m // nseq
    nt = seq // tq
    with_router = prm["router_w"] is not None
    cur = lambda b, t: (b * nt + t, 0)
    fixed2 = lambda b, t: (0, 0)
    fixed3 = lambda b, t: (0, 0, 0)
    if has_past:
        kvp_map = lambda b, t: (b, 0)
        up_map = lambda b, t: (b, 0)
    else:
        kvp_map = lambda b, t: (jnp.maximum(b * (seq // WINDOW) + t * (tq // WINDOW) - 1, 0), 0)
        up_map = lambda b, t: (jnp.maximum(b * (seq // CONV_PAD) + t * (tq // CONV_PAD) - 1, 0), 0)
    in_specs = [
        pl.BlockSpec((tq, ATTN_WIDTH), cur),
        pl.BlockSpec((tq, 2 * KV_WIDTH), cur),
        pl.BlockSpec((WINDOW, 2 * KV_WIDTH), kvp_map),
        pl.BlockSpec((tq, CONV_CH), cur),
        pl.BlockSpec((CONV_PAD, CONV_CH), up_map),
        pl.BlockSpec((tq, D_MODEL), cur),
        pl.BlockSpec((N_KV_HEADS, BAND, GROUP * CHUNK), fixed3),
        pl.BlockSpec((N_KV_HEADS, 1, GROUP * CHUNK), fixed3),
        pl.BlockSpec((CONV_PAD, CONV_CH), fixed2),
        pl.BlockSpec((1, CONV_CH), fixed2),
        pl.BlockSpec((1, CONV_CH), fixed2),
        pl.BlockSpec((1, CONV_CH), fixed2),
        pl.BlockSpec((1, ATTN_WIDTH), fixed2),
        pl.BlockSpec((1, CONV_CH), fixed2),
        pl.BlockSpec((D_MODEL, D_MODEL), fixed2),
        pl.BlockSpec((1, D_MODEL), fixed2),
    ]
    args = [q, kv, kv_prev, u, u_prev, h, prm["bias"], prm["sinks"], prm["conv_w"], prm["conv_b"],
            prm["cln_g"], prm["cln_b"], prm["on_attn"], prm["on_conv"], prm["w_out"], prm["norm_ffn"]]
    out_specs = [pl.BlockSpec((tq, D_MODEL), cur), pl.BlockSpec((tq, D_MODEL), cur)]
    out_shape = [jax.ShapeDtypeStruct((m, D_MODEL), F32),
                 jax.ShapeDtypeStruct((m, D_MODEL), F32 if with_router else BF16)]
    scratch = [
        pltpu.VMEM((WINDOW + tq, 2 * KV_WIDTH), BF16),
        pltpu.VMEM((SUBLANES, CONV_PAD + tq, CONV_CH), F32),
        pltpu.VMEM((tq, ATTN_WIDTH), F32),
        pltpu.VMEM((tq, CONV_CH), F32),
        pltpu.VMEM((tq // CHUNK * (N_HEADS // HEAD_PACK), BAND, HEAD_PACK * CHUNK), F32),
        pltpu.VMEM((tq // CHUNK * (N_HEADS // HEAD_PACK), BAND, HEAD_PACK * CHUNK), BF16),
    ]
    if with_router:
        utri = (jnp.arange(tq, dtype=jnp.int32)[:, None] < jnp.arange(tq, dtype=jnp.int32)[None, :]).astype(BF16)
        in_specs += [pl.BlockSpec((D_MODEL, 2 * LANES), fixed2), pl.BlockSpec((1, LANES), fixed2),
                     pl.BlockSpec((tq, tq), fixed2)]
        args += [prm["router_w"], prm["router_b"], utri]
        out_specs += [pl.BlockSpec((tq, LANES), cur),
                      pl.BlockSpec((1, SUBLANES, tq), lambda b, t: (b * nt + t, 0, 0)),
                      pl.BlockSpec((SUBLANES, LANES), fixed2)]
        out_shape += [jax.ShapeDtypeStruct((m, LANES), F32),
                      jax.ShapeDtypeStruct((nseq * nt, SUBLANES, tq), F32),
                      jax.ShapeDtypeStruct((SUBLANES, LANES), F32)]
        scratch.append(pltpu.VMEM((SUBLANES, LANES), F32))
    return pl.pallas_call(
        functools.partial(_mixer_kernel, tq=tq, has_past=has_past, with_router=with_router),
        grid=(nseq, nt),
        in_specs=in_specs,
        out_specs=out_specs,
        out_shape=out_shape,
        scratch_shapes=scratch,
        compiler_params=pltpu.CompilerParams(
            dimension_semantics=("arbitrary", "arbitrary"), vmem_limit_bytes=VMEM_LIMIT),
        name="mixer_past" if has_past else "mixer",
    )(*args)


def _ple(h2, p_ref, npl_ref, wpg_ref, wpl_ref):
    r = _rms(h2, npl_ref[...]).astype(BF16)
    gate = jax.nn.sigmoid(jnp.dot(r, wpg_ref[...], preferred_element_type=F32))
    pe = jnp.dot(p_ref[...].astype(BF16), wpl_ref[...], preferred_element_type=F32)
    return h2 + gate * pe


def _swiglu(x, wg, wu, wd):
    g = jnp.dot(x, wg, preferred_element_type=F32)
    u = jnp.dot(x, wu, preferred_element_type=F32)
    a = (g * jax.nn.sigmoid(g) * u).astype(BF16)
    return jnp.dot(a, wd, preferred_element_type=F32)


def _ffn_kernel(x_ref, h_ref, p_ref, wg_ref, wu_ref, wd_ref, npl_ref, wpg_ref, wpl_ref, o_ref):
    f = pl.program_id(1)

    @pl.when(f == 0)
    def _():
        o_ref[...] = h_ref[...]

    o_ref[...] += _swiglu(x_ref[...], wg_ref[...], wu_ref[...], wd_ref[...])

    @pl.when(f == pl.num_programs(1) - 1)
    def _():
        o_ref[...] = _ple(o_ref[...], p_ref, npl_ref, wpg_ref, wpl_ref)


def _ffn(x, h, p, layer, wg, wu, wd, npl, wpg, wpl, tm, tf):
    m = x.shape[0]
    row = lambda i, f: (i, 0)
    fixed = lambda i, f: (0, 0)
    return pl.pallas_call(
        _ffn_kernel,
        grid=(m // tm, D_FF // tf),
        in_specs=[
            pl.BlockSpec((tm, D_MODEL), row),
            pl.BlockSpec((tm, D_MODEL), row),
            pl.BlockSpec((None, tm, D_PLE), lambda i, f: (layer, i, 0)),
            pl.BlockSpec((D_MODEL, tf), lambda i, f: (0, f)),
            pl.BlockSpec((D_MODEL, tf), lambda i, f: (0, f)),
            pl.BlockSpec((tf, D_MODEL), lambda i, f: (f, 0)),
            pl.BlockSpec((1, D_MODEL), fixed),
            pl.BlockSpec((D_MODEL, D_MODEL), fixed),
            pl.BlockSpec((D_PLE, D_MODEL), fixed),
        ],
        out_specs=pl.BlockSpec((tm, D_MODEL), row),
        out_shape=jax.ShapeDtypeStruct((m, D_MODEL), F32),
        compiler_params=pltpu.CompilerParams(
            dimension_semantics=("arbitrary", "arbitrary"), vmem_limit_bytes=VMEM_LIMIT),
        name="ffn",
    )(x, h, p, wg, wu, wd, npl, wpg, wpl)


def _row_copy(src_ref, src_row, dst_ref, dst_row, sem):
    return pltpu.make_async_copy(src_ref.at[pl.ds(src_row, 1)], dst_ref.at[pl.ds(dst_row, 1)], sem)


def _dispatch_kernel(zstart_ref, zflag_ref, dest_ref, x_ref, xs_ref, zero_ref, zsem, sem, *, tm, tg, tiles):
    i = pl.program_id(0)

    @pl.when(i == 0)
    def _():
        zero_ref[...] = jnp.zeros_like(zero_ref)
        def fill(e):
            return pltpu.make_async_copy(zero_ref, xs_ref.at[pl.ds(pl.multiple_of(zstart_ref[e], tg), tg)], zsem)

        for e in range(N_FILL):
            @pl.when(zflag_ref[e] > 0)
            def _():
                fill(e).start()
        for e in range(N_FILL):
            @pl.when(zflag_ref[e] > 0)
            def _():
                fill(e).wait()

    for k in range(tiles):
        def issue(g, carry, k=k):
            base = pl.multiple_of(g * DMA_ROWS, DMA_ROWS)
            rows = x_ref.at[pl.ds(k * tm + base, DMA_ROWS)]
            for j in range(DMA_ROWS):
                for s in range(TOP_K):
                    _row_copy(rows, j, xs_ref, dest_ref[k, 0, s * tm + base + j], sem).start()
            return carry

        lax.fori_loop(0, tm // DMA_ROWS, issue, 0)

    for s in range(TOP_K):
        pltpu.make_async_copy(x_ref, xs_ref.at[pl.ds(0, tiles * tm)], sem).wait()


def _dispatch(x, dest, zstart, zflag, n_rows, tm, tg):
    m = x.shape[0]
    tiles = DISPATCH_TILES if (m // tm) % DISPATCH_TILES == 0 else 1
    return pl.pallas_call(
        functools.partial(_dispatch_kernel, tm=tm, tg=tg, tiles=tiles),
        grid_spec=pltpu.PrefetchScalarGridSpec(
            num_scalar_prefetch=2,
            grid=(m // (tiles * tm),),
            in_specs=[
                pl.BlockSpec((tiles, 1, TOP_K * tm), lambda i, zs, zf: (i, 0, 0), memory_space=pltpu.SMEM),
                pl.BlockSpec((tiles * tm, D_MODEL), lambda i, zs, zf: (i, 0)),
            ],
            out_specs=pl.BlockSpec(memory_space=pl.ANY),
            scratch_shapes=[
                pltpu.VMEM((tg, D_MODEL), F32),
                pltpu.SemaphoreType.DMA(()),
                pltpu.SemaphoreType.DMA(()),
            ],
        ),
        out_shape=jax.ShapeDtypeStruct((n_rows, D_MODEL), F32),
        compiler_params=pltpu.CompilerParams(
            dimension_semantics=("arbitrary",), vmem_limit_bytes=VMEM_LIMIT),
        name="moe_dispatch",
    )(zstart, zflag, dest, x)


def _moe_ffn_kernel(te_ref, nused_ref, x_ref, wg_ref, wu_ref, wd_ref, o_ref):
    i = pl.program_id(0)
    f = pl.program_id(1)

    @pl.when((i >= nused_ref[0]) & (f == 0))
    def _():
        o_ref[...] = jnp.zeros_like(o_ref)

    @pl.when(i < nused_ref[0])
    def _():
        y = _swiglu(x_ref[...].astype(BF16), wg_ref[0], wu_ref[0], wd_ref[0])

        @pl.when(f == 0)
        def _():
            o_ref[...] = y

        @pl.when(f > 0)
        def _():
            o_ref[...] += y


def _moe_ffn(xs, te, nused, wg, wu, wd, tg, tf):
    n_rows = xs.shape[0]
    nf = D_FF // tf
    tile = lambda i, nu: jnp.minimum(i, nu[0] - 1)
    fstep = lambda i, f, nu: jnp.where(i < nu[0], f, nf - 1)
    return pl.pallas_call(
        _moe_ffn_kernel,
        grid_spec=pltpu.PrefetchScalarGridSpec(
            num_scalar_prefetch=2,
            grid=(n_rows // tg, nf),
            in_specs=[
                pl.BlockSpec((tg, D_MODEL), lambda i, f, te, nu: (tile(i, nu), 0)),
                pl.BlockSpec((1, D_MODEL, tf), lambda i, f, te, nu: (te[tile(i, nu)], 0, fstep(i, f, nu))),
                pl.BlockSpec((1, D_MODEL, tf), lambda i, f, te, nu: (te[tile(i, nu)], 0, fstep(i, f, nu))),
                pl.BlockSpec((1, tf, D_MODEL), lambda i, f, te, nu: (te[tile(i, nu)], fstep(i, f, nu), 0)),
            ],
            out_specs=pl.BlockSpec((tg, D_MODEL), lambda i, f, te, nu: (i, 0)),
        ),
        out_shape=jax.ShapeDtypeStruct((n_rows, D_MODEL), F32),
        compiler_params=pltpu.CompilerParams(
            dimension_semantics=("arbitrary", "arbitrary"), vmem_limit_bytes=VMEM_LIMIT),
        name="moe_ffn",
    )(te, nused, xs, wg, wu, wd)


def _combine_kernel(dest_ref, dest_next_ref, route_ref, h_ref, p_ref, npl_ref, wpg_ref, wpl_ref, ys_ref, o_ref,
                    buf_ref, sem, *, tm):
    i = pl.program_id(0)
    cur = i % 2

    def issue_rows(rows_ref, half, base):
        for s in range(TOP_K):
            rows = buf_ref.at[half, s, pl.ds(base, DMA_ROWS)]
            for j in range(DMA_ROWS):
                _row_copy(ys_ref, rows_ref[0, 0, s * tm + base + j], rows, j, sem.at[half]).start()

    def wait_half(half):
        for s in range(TOP_K):
            pltpu.make_async_copy(ys_ref.at[pl.ds(0, tm)], buf_ref.at[half, s], sem.at[half]).wait()

    @pl.when(i == 0)
    def _():
        def issue(g, carry):
            issue_rows(dest_ref, 0, pl.multiple_of(g * DMA_ROWS, DMA_ROWS))
            return carry

        lax.fori_loop(0, tm // DMA_ROWS, issue, 0)

    wait_half(cur)

    g1 = route_ref[:, 2:3]
    g2 = route_ref[:, 3:4]
    h2 = h_ref[...] + (g1 * buf_ref[cur, 0] + g2 * buf_ref[cur, 1])

    for g in range(tm // DMA_ROWS):
        issue_rows(dest_next_ref, 1 - cur, g * DMA_ROWS)

    o_ref[...] = _ple(h2, p_ref, npl_ref, wpg_ref, wpl_ref)

    @pl.when(i == pl.num_programs(0) - 1)
    def _():
        wait_half(1 - cur)


def _combine(ys, dest, route, h, p, layer, npl, wpg, wpl, tm):
    m = h.shape[0]
    row = lambda i: (i, 0)
    fixed = lambda i: (0, 0)
    return pl.pallas_call(
        functools.partial(_combine_kernel, tm=tm),
        grid=(m // tm,),
        in_specs=[
            pl.BlockSpec((1, 1, TOP_K * tm), lambda i: (i, 0, 0), memory_space=pltpu.SMEM),
            pl.BlockSpec((1, 1, TOP_K * tm), lambda i: (jnp.minimum(i + 1, m // tm - 1), 0, 0),
                         memory_space=pltpu.SMEM),
            pl.BlockSpec((tm, LANES), row),
            pl.BlockSpec((tm, D_MODEL), row),
            pl.BlockSpec((None, tm, D_PLE), lambda i: (layer, i, 0)),
            pl.BlockSpec((1, D_MODEL), fixed),
            pl.BlockSpec((D_MODEL, D_MODEL), fixed),
            pl.BlockSpec((D_PLE, D_MODEL), fixed),
            pl.BlockSpec(memory_space=pl.ANY),
        ],
        out_specs=pl.BlockSpec((tm, D_MODEL), row),
        out_shape=jax.ShapeDtypeStruct((m, D_MODEL), F32),
        scratch_shapes=[
            pltpu.VMEM((2, TOP_K, tm, D_MODEL), F32),
            pltpu.SemaphoreType.DMA((2,)),
        ],
        compiler_params=pltpu.CompilerParams(
            dimension_semantics=("arbitrary",), vmem_limit_bytes=VMEM_LIMIT),
        name="moe_combine",
    )(dest, dest, route, h, p, npl, wpg, wpl, ys)


def _routing(route_t, count, m, tm, tg):
    rows = jnp.transpose(route_t, (1, 0, 2)).reshape(SUBLANES, m)
    counts = count[:N_EXPERTS, 0].astype(jnp.int32)
    padded = (counts + tg - 1) // tg * tg
    ends = jnp.cumsum(padded)
    offs = ends - padded

    def dest_rows(expert, rank):
        expert = expert.astype(jnp.int32)
        start = sum(jnp.where(expert == k, offs[k], 0) for k in range(N_EXPERTS))
        return (start + rank.astype(jnp.int32)).reshape(m // tm, 1, tm)

    dest = jnp.concatenate([dest_rows(rows[s], rows[2 * TOP_K + s]) for s in range(TOP_K)], axis=2)
    n_tiles = TOP_K * m // tg + N_EXPERTS
    nused = ends[-1] // tg
    tile_ids = jnp.minimum(jnp.arange(n_tiles, dtype=jnp.int32), nused - 1)
    te = jnp.sum((tile_ids[:, None] >= (ends // tg)[None, :]).astype(jnp.int32), axis=1)
    te = jnp.minimum(te, N_EXPERTS - 1)
    slack = nused + jnp.arange(N_FILL - N_EXPERTS, dtype=jnp.int32)
    zstart = jnp.concatenate([jnp.maximum(ends - tg, 0), jnp.minimum(slack, n_tiles - 1) * tg])
    zflag = jnp.concatenate([counts > 0, slack < n_tiles]).astype(jnp.int32)
    return (dest, te.astype(jnp.int32), nused.reshape(1).astype(jnp.int32),
            zstart.astype(jnp.int32), zflag, n_tiles * tg)


def _moe(hn, h1, p, layer, route, route_t, count, wg, wu, wd, npl, wpg, wpl, tm, tg, tf):
    dest, te, nused, zstart, zflag, n_rows = _routing(route_t, count, hn.shape[0], tm, tg)
    xs = _dispatch(hn, dest, zstart, zflag, n_rows, tm, tg)
    ys = _moe_ffn(xs, te, nused, wg, wu, wd, tg, tf)
    return _combine(ys, dest, route, h1, p, layer, npl, wpg, wpl, tm)


def _alibi_bias():
    slopes = jnp.exp2(-8.0 * jnp.arange(1, N_HEADS + 1, dtype=F32) / N_HEADS)
    qi = jnp.arange(CHUNK, dtype=jnp.int32)[None, :]
    kj = jnp.arange(BAND, dtype=jnp.int32)[:, None]
    dist = jnp.abs(qi + WINDOW - kj).astype(F32)
    bias = slopes[:, None, None] * dist[None]
    bias = bias.reshape(N_KV_HEADS, GROUP, BAND, CHUNK)
    return jnp.transpose(bias, (0, 2, 1, 3)).reshape(N_KV_HEADS, BAND, GROUP * CHUNK)


def _head_sum_matrix():
    head = jnp.arange(QK_WIDTH, dtype=jnp.int32) // HEAD_DIM
    return (head[:, None] == head[None, :]).astype(BF16)


def _layer_params(l, norm_mix, w_in, q_gain, k_gain, attn_sinks, conv_w, conv_b, conv_ln_g, conv_ln_b,
                  out_norm_attn, out_norm_conv, w_out, norm_ffn, router_w, router_b, ple_norm,
                  w_ple_gate, w_ple):
    scale = HEAD_DIM ** -0.5
    prm = {
        "norm_mix": norm_mix[l][None, :],
        "w_in": w_in[l].astype(BF16),
        "qk_gain": jnp.concatenate([jnp.tile(q_gain[l] * scale, N_HEADS), jnp.tile(k_gain[l], N_KV_HEADS)])[None, :],
        "sinks": jnp.broadcast_to(attn_sinks[l].reshape(N_KV_HEADS, 1, GROUP, 1),
                                  (N_KV_HEADS, 1, GROUP, CHUNK)).reshape(N_KV_HEADS, 1, GROUP * CHUNK),
        "conv_w": jnp.pad(conv_w[l], ((0, CONV_PAD - CONV_K), (0, 0))),
        "conv_b": conv_b[l][None, :],
        "cln_g": conv_ln_g[l][None, :],
        "cln_b": conv_ln_b[l][None, :],
        "on_attn": out_norm_attn[l][None, :],
        "on_conv": out_norm_conv[l][None, :],
        "w_out": w_out[l].astype(BF16),
        "norm_ffn": norm_ffn[l][None, :],
        "ple_norm": ple_norm[l][None, :],
        "w_ple_gate": w_ple_gate[l].astype(BF16),
        "w_ple": w_ple[l].astype(BF16),
        "router_w": None,
        "router_b": None,
    }
    if l % 2 == 1:
        i = l // 2
        rw = jnp.pad(router_w[i], ((0, 0), (0, LANES - N_EXPERTS)))
        rw_hi = rw.astype(BF16)
        rw_lo = (rw - rw_hi.astype(F32)).astype(BF16)
        prm["router_w"] = jnp.concatenate([rw_hi, rw_lo], axis=1)
        prm["router_b"] = jnp.pad(router_b[i], (0, LANES - N_EXPERTS))[None, :]
    return prm


def _trunk(x, p, cache_k, cache_v, state_conv, layers, ffn_w, bias, hsum, *, tm, tq):
    nseq, seq, _ = x.shape
    m = nseq * seq
    has_past = cache_k is not None
    h = x.reshape(m, D_MODEL)
    p2 = p.reshape(p.shape[0], m, D_PLE)
    win_k, win_v, convs = [], [], []
    for l, prm in enumerate(layers):
        q, kv, u = _in_proj(h, prm["norm_mix"], prm["w_in"], prm["qk_gain"], hsum, tm)
        if has_past:
            win = cache_k.shape[2]
            kv_prev = jnp.concatenate([cache_k[l].reshape(nseq * win, KV_WIDTH),
                                       cache_v[l].reshape(nseq * win, KV_WIDTH)], axis=1)
            u_prev = jnp.pad(state_conv[l], ((0, 0), (CONV_PAD - (CONV_K - 1), 0), (0, 0))).reshape(
                nseq * CONV_PAD, CONV_CH)
        else:
            kv_prev, u_prev = kv, u
        outs = _mixer(q, kv, kv_prev, u, u_prev, h, dict(prm, bias=bias), nseq=nseq, tq=tq, has_past=has_past)
        h1, hn = outs[0], outs[1]
        wg, wu, wd = ffn_w[l]
        if len(outs) > 2:
            route, route_t, count = outs[2:]
            h = _moe(hn, h1, p2, l, route, route_t, count, wg, wu, wd,
                     prm["ple_norm"], prm["w_ple_gate"], prm["w_ple"], tm, tm, D_FF // 2)
        else:
            h = _ffn(hn, h1, p2, l, wg, wu, wd, prm["ple_norm"], prm["w_ple_gate"], prm["w_ple"], tm, D_FF // 2)
        kv3 = kv.reshape(nseq, seq, 2 * KV_WIDTH)
        u3 = u.reshape(nseq, seq, CONV_CH)
        if has_past:
            kv3 = jnp.concatenate([kv_prev.reshape(nseq, win, 2 * KV_WIDTH), kv3], axis=1)[:, -win:]
            u3 = jnp.concatenate([state_conv[l], u3], axis=1)
        else:
            kv3 = kv3[:, seq - WINDOW:]
        win_k.append(kv3[..., :KV_WIDTH].reshape(nseq, -1, N_KV_HEADS, HEAD_DIM))
        win_v.append(kv3[..., KV_WIDTH:].reshape(nseq, -1, N_KV_HEADS, HEAD_DIM))
        convs.append(u3[:, -(CONV_K - 1):])
    return h.reshape(nseq, seq, D_MODEL), jnp.stack(win_k), jnp.stack(win_v), jnp.stack(convs)


def kernel(x_prompt, x_sample, p_prompt, p_sample, cache_k, cache_v, state_conv, norm_mix, w_in, q_gain, k_gain, attn_sinks, conv_w, conv_b, conv_ln_g, conv_ln_b, out_norm_attn, out_norm_conv, w_out, norm_ffn, ffn_gate, ffn_up, ffn_down, router_w, router_b, moe_gate, moe_up, moe_down, ple_norm, w_ple_gate, w_ple):
    depth = w_in.shape[0]
    layers = [
        _layer_params(l, norm_mix, w_in, q_gain, k_gain, attn_sinks, conv_w, conv_b, conv_ln_g, conv_ln_b,
                      out_norm_attn, out_norm_conv, w_out, norm_ffn, router_w, router_b, ple_norm,
                      w_ple_gate, w_ple)
        for l in range(depth)
    ]
    ffn_w = []
    for l in range(depth):
        i = l // 2
        if l % 2 == 0:
            ffn_w.append((ffn_gate[i].astype(BF16), ffn_up[i].astype(BF16), ffn_down[i].astype(BF16)))
        else:
            ffn_w.append((moe_gate[i].astype(BF16), moe_up[i].astype(BF16), moe_down[i].astype(BF16)))
    bias = _alibi_bias()
    hsum = _head_sum_matrix()
    tm_p = min(512, x_prompt.shape[0] * x_prompt.shape[1])
    tq_p = min(512, x_prompt.shape[1])
    y_p, wk_p, wv_p, cv_p = _trunk(x_prompt, p_prompt, None, None, None, layers, ffn_w, bias, hsum,
                                   tm=tm_p, tq=tq_p)
    tm_s = min(512, x_sample.shape[0] * x_sample.shape[1])
    y_s, wk_s, wv_s, cv_s = _trunk(x_sample, p_sample, cache_k, cache_v, state_conv, layers, ffn_w, bias, hsum,
                                   tm=tm_s, tq=x_sample.shape[1])
    return (y_p, y_s, wk_p, wv_p, cv_p, wk_s, wv_s, cv_s)
```

```python
import functools

import jax
import jax.numpy as jnp
from jax import lax
from jax.experimental import pallas as pl
from jax.experimental.pallas import tpu as pltpu

D_MODEL = 1024
CHUNK = 64
WINDOW = 128
N_HEADS = 8
N_KV_HEADS = 2
HEAD_DIM = 64
GROUP = N_HEADS // N_KV_HEADS
HEAD_PACK = 4
ATTN_WIDTH = N_HEADS * HEAD_DIM
KV_WIDTH = N_KV_HEADS * HEAD_DIM
QK_WIDTH = ATTN_WIDTH + KV_WIDTH
CONV_CH = 512
CONV_K = 31
SUBLANES = 8
CONV_PAD = 32
CONV_ROWS = 64
CONV_ACCS = 2
D_IN = ATTN_WIDTH + 2 * KV_WIDTH + 2 * CONV_CH
BAND = WINDOW + CHUNK
D_FF = 2816
N_EXPERTS = 8
TOP_K = 2
N_FILL = 2 * N_EXPERTS
D_PLE = 256
DMA_ROWS = 8
DISPATCH_TILES = 2
EPS = 1e-6
LANES = 128

F32 = jnp.float32
BF16 = jnp.bfloat16

VMEM_LIMIT = 56 * 1024 * 1024

def _rms(x, g):
    return x * lax.rsqrt(jnp.mean(x * x, axis=-1, keepdims=True) + EPS) * g


def _in_proj_kernel(h_ref, nrm_ref, w_ref, gain_ref, hsum_ref, q_ref, kv_ref, u_ref):
    xn = _rms(h_ref[...], nrm_ref[...])
    z = jnp.dot(xn.astype(BF16), w_ref[...], preferred_element_type=F32)
    qk = z[:, :QK_WIDTH]
    ss = jnp.dot((qk * qk).astype(BF16), hsum_ref[...], preferred_element_type=F32)
    qkn = qk * lax.rsqrt(ss * (1.0 / HEAD_DIM) + EPS) * gain_ref[...]
    q_ref[...] = qkn[:, :ATTN_WIDTH].astype(q_ref.dtype)
    kv_ref[:, :KV_WIDTH] = qkn[:, ATTN_WIDTH:]
    kv_ref[:, KV_WIDTH:] = z[:, QK_WIDTH:QK_WIDTH + KV_WIDTH]
    a = z[:, QK_WIDTH + KV_WIDTH:QK_WIDTH + KV_WIDTH + CONV_CH]
    gl = z[:, QK_WIDTH + KV_WIDTH + CONV_CH:]
    u_ref[...] = a * jax.nn.sigmoid(gl)


def _in_proj(h, nrm, w_in, qk_gain, hsum, tm):
    m = h.shape[0]
    row = lambda i: (i, 0)
    fixed = lambda i: (0, 0)
    return pl.pallas_call(
        _in_proj_kernel,
        grid=(m // tm,),
        in_specs=[
            pl.BlockSpec((tm, D_MODEL), row),
            pl.BlockSpec((1, D_MODEL), fixed),
            pl.BlockSpec((D_MODEL, D_IN), fixed),
            pl.BlockSpec((1, QK_WIDTH), fixed),
            pl.BlockSpec((QK_WIDTH, QK_WIDTH), fixed),
        ],
        out_specs=[
            pl.BlockSpec((tm, ATTN_WIDTH), row),
            pl.BlockSpec((tm, 2 * KV_WIDTH), row),
            pl.BlockSpec((tm, CONV_CH), row),
        ],
        out_shape=[
            jax.ShapeDtypeStruct((m, ATTN_WIDTH), BF16),
            jax.ShapeDtypeStruct((m, 2 * KV_WIDTH), F32),
            jax.ShapeDtypeStruct((m, CONV_CH), F32),
        ],
        compiler_params=pltpu.CompilerParams(
            dimension_semantics=("arbitrary",), vmem_limit_bytes=VMEM_LIMIT),
        name="in_proj",
    )(h, nrm, w_in, qk_gain, hsum)


def _mixer_kernel(q_ref, kv_ref, kvp_ref, u_ref, up_ref, h_ref, bias_ref, sink_ref,
                  cw_ref, cb_ref, lng_ref, lnb_ref, ona_ref, onc_ref, wout_ref, nffn_ref,
                  *rest, tq, nb, has_past, with_router):
    if with_router:
        (rw_ref, rb_ref, utri_ref, h1_ref, hn_ref, route_ref, routet_ref, count_ref,
         kvx_ref, ux_ref, oa_ref, cv_ref, s_ref, p_ref, cnt_ref) = rest
    else:
        h1_ref, hn_ref, kvx_ref, ux_ref, oa_ref, cv_ref, s_ref, p_ref = rest
    t = pl.program_id(1)
    nch = tq // CHUNK
    kv_seg = WINDOW + tq
    u_seg = CONV_PAD + tq

    for i in range(nb):
        kvx_ref[i * kv_seg:i * kv_seg + WINDOW, :] = kvp_ref[i * WINDOW:(i + 1) * WINDOW, :].astype(BF16)
        kvx_ref[i * kv_seg + WINDOW:(i + 1) * kv_seg, :] = kv_ref[i * tq:(i + 1) * tq, :].astype(BF16)

    packs = N_HEADS // HEAD_PACK
    units = [(i * tq + c * CHUNK, i * kv_seg + c * CHUNK, c, hp)
             for i in range(nb) for c in range(nch) for hp in range(packs)]

    def unit_operands(hp):
        j = hp * HEAD_PACK // GROUP
        lanes = slice((hp * HEAD_PACK % GROUP) * CHUNK, (hp * HEAD_PACK % GROUP + HEAD_PACK) * CHUNK)
        return j, lanes

    for u, (qrow, krow, c, hp) in enumerate(units):
        j, lanes = unit_operands(hp)
        qc = q_ref[qrow:qrow + CHUNK, :]
        qs = jnp.concatenate([qc[:, h * HEAD_DIM:(h + 1) * HEAD_DIM]
                              for h in range(hp * HEAD_PACK, (hp + 1) * HEAD_PACK)],
                             axis=0)
        kj = kvx_ref[krow:krow + BAND, j * HEAD_DIM:(j + 1) * HEAD_DIM]
        s = lax.dot_general(kj, qs, (((1,), (1,)), ((), ())), preferred_element_type=F32)
        s = s - bias_ref[j, :, lanes]
        if not has_past and c < WINDOW // CHUNK:
            kpos = lax.broadcasted_iota(jnp.int32, (BAND, 1), 0) + (t * tq + c * CHUNK - WINDOW)
            s = jnp.where(kpos >= 0, s, -jnp.inf)
        s_ref[u] = s

    for u, (qrow, krow, c, hp) in enumerate(units):
        j, lanes = unit_operands(hp)
        s = s_ref[u]
        sink = sink_ref[j, :, lanes]
        mx = jnp.maximum(jnp.max(s, axis=0, keepdims=True), sink)
        e = jnp.exp(s - mx)
        den = jnp.sum(e, axis=0, keepdims=True) + jnp.exp(sink - mx)
        p_ref[u] = (e * (1.0 / den)).astype(BF16)

    for u0 in range(0, len(units), packs):
        outs = []
        for u in range(u0, u0 + packs):
            qrow, krow, c, hp = units[u]
            j, _ = unit_operands(hp)
            vj = kvx_ref[krow:krow + BAND, KV_WIDTH + j * HEAD_DIM:KV_WIDTH + (j + 1) * HEAD_DIM]
            o = lax.dot_general(p_ref[u], vj, (((0,), (0,)), ((), ())),
                                preferred_element_type=F32)
            outs.extend(o[g * CHUNK:(g + 1) * CHUNK, :] for g in range(HEAD_PACK))
        oa_ref[qrow:qrow + CHUNK, :] = jnp.concatenate(outs, axis=1)

    for i in range(nb):
        prefix = up_ref[i * CONV_PAD:(i + 1) * CONV_PAD, :]
        ux_ref[0, i * u_seg:i * u_seg + CONV_PAD, :] = prefix if has_past else jnp.where(t > 0, prefix, 0.0)
        ux_ref[0, i * u_seg + CONV_PAD:(i + 1) * u_seg, :] = u_ref[i * tq:(i + 1) * tq, :]
    shifted_rows = nb * u_seg - SUBLANES
    for r in range(1, SUBLANES):
        ux_ref[r, :shifted_rows, :] = ux_ref[0, r:r + shifted_rows, :]
    lead = CONV_PAD - (CONV_K - 1)

    for cg in range(CONV_CH // LANES):
        lanes = slice(cg * LANES, (cg + 1) * LANES)
        taps = [cw_ref[k:k + 1, lanes] for k in range(CONV_K)]

        def conv_rows(i, carry, lanes=lanes, taps=taps):
            out0 = pl.multiple_of(i * CONV_ROWS, CONV_ROWS)
            r0 = out0 if nb == 1 else pl.multiple_of(i * u_seg, SUBLANES)
            accs = [None] * CONV_ACCS
            for r in range(SUBLANES):
                steps = [(k, (lead + k) // SUBLANES) for k in range(CONV_K) if (lead + k) % SUBLANES == r]
                slab = ux_ref[r, pl.ds(r0, CONV_ROWS + steps[-1][1] * SUBLANES), lanes]
                for k, a in steps:
                    term = slab[a * SUBLANES:a * SUBLANES + CONV_ROWS, :] * taps[k]
                    accs[k % CONV_ACCS] = term if accs[k % CONV_ACCS] is None else accs[k % CONV_ACCS] + term
            cv_ref[pl.ds(out0, CONV_ROWS), lanes] = functools.reduce(lambda x, y: x + y, accs) + cb_ref[:, lanes]
            return carry

        lax.fori_loop(0, nb * tq // CONV_ROWS, conv_rows, 0)

    cv = cv_ref[...]
    mu = jnp.mean(cv, axis=-1, keepdims=True)
    xc = cv - mu
    ln = xc * lax.rsqrt(jnp.mean(xc * xc, axis=-1, keepdims=True) + EPS) * lng_ref[...] + lnb_ref[...]
    oc = ln * jax.nn.sigmoid(ln)
    cn = _rms(oc, onc_ref[...]).astype(BF16)

    an = _rms(oa_ref[...], ona_ref[...]).astype(BF16)
    mixed = (jnp.dot(an, wout_ref[:ATTN_WIDTH, :], preferred_element_type=F32)
             + jnp.dot(cn, wout_ref[ATTN_WIDTH:, :], preferred_element_type=F32))
    h1 = h_ref[...] + mixed
    h1_ref[...] = h1
    hn = _rms(h1, nffn_ref[...])
    hn_ref[...] = hn.astype(hn_ref.dtype)

    if with_router:
        hi = hn.astype(BF16)
        lo = (hn - hi.astype(F32)).astype(BF16)
        part = jnp.dot(hi, rw_ref[...], preferred_element_type=F32)
        logits = (part[:, :LANES] + part[:, LANES:]
                  + jnp.dot(lo, rw_ref[:, :LANES], preferred_element_type=F32) + rb_ref[...])
        lt = logits.T[:N_EXPERTS, :]
        row = lax.broadcasted_iota(jnp.int32, lt.shape, 0)
        m1 = jnp.max(lt, axis=0, keepdims=True)
        i1 = jnp.min(jnp.where(lt == m1, row, N_EXPERTS), axis=0, keepdims=True)
        rest_l = jnp.where(row == i1, -jnp.inf, lt)
        m2 = jnp.max(rest_l, axis=0, keepdims=True)
        i2 = jnp.min(jnp.where(rest_l == m2, row, N_EXPERTS), axis=0, keepdims=True)
        e2 = jnp.exp(m2 - m1)
        den = 1.0 + e2

        @pl.when((pl.program_id(0) == 0) & (t == 0))
        def _():
            cnt_ref[...] = jnp.zeros_like(cnt_ref)

        sel = ((row == i1) | (row == i2)).astype(F32)
        ahead = jnp.dot(sel.astype(BF16), utri_ref[...], preferred_element_type=F32) + cnt_ref[:, 0:1]
        r1 = jnp.sum(jnp.where(row == i1, ahead, 0.0), axis=0, keepdims=True)
        r2 = jnp.sum(jnp.where(row == i2, ahead, 0.0), axis=0, keepdims=True)
        cnt_ref[...] += jnp.sum(sel, axis=1, keepdims=True)
        count_ref[...] = cnt_ref[...]

        route_t = jnp.concatenate(
            [i1.astype(F32), i2.astype(F32), 1.0 / den, e2 / den, r1, r2,
             jnp.zeros((SUBLANES - 3 * TOP_K, nb * tq), F32)], axis=0)
        routet_ref[0] = route_t
        route_ref[...] = jnp.concatenate([route_t, jnp.zeros((LANES - SUBLANES, nb * tq), F32)], axis=0).T


def _mixer(q, kv, kv_prev, u, u_prev, h, prm, *, nseq, tq, nb, has_past):
    m = h.shape[0]
    seq = m // nseq
    nt = seq // tq
    assert nb == 1 or (has_past and nt == 1 and tq == CONV_ROWS and nseq % nb == 0)
    tr = nb * tq
    with_router = prm["router_w"] is not None
    cur = lambda b, t: (b * nt + t, 0)
    fixed2 = lambda b, t: (0, 0)
    fixed3 = lambda b, t: (0, 0, 0)
    if has_past:
        kvp_map = lambda b, t: (b, 0)
        up_map = lambda b, t: (b, 0)
    else:
        kvp_map = lambda b, t: (jnp.maximum(b * (seq // WINDOW) + t * (tq // WINDOW) - 1, 0), 0)
        up_map = lambda b, t: (jnp.maximum(b * (seq // CONV_PAD) + t * (tq // CONV_PAD) - 1, 0), 0)
    in_specs = [
        pl.BlockSpec((tr, ATTN_WIDTH), cur),
        pl.BlockSpec((tr, 2 * KV_WIDTH), cur),
        pl.BlockSpec((nb * WINDOW, 2 * KV_WIDTH), kvp_map),
        pl.BlockSpec((tr, CONV_CH), cur),
        pl.BlockSpec((nb * CONV_PAD, CONV_CH), up_map),
        pl.BlockSpec((tr, D_MODEL), cur),
        pl.BlockSpec((N_KV_HEADS, BAND, GROUP * CHUNK), fixed3),
        pl.BlockSpec((N_KV_HEADS, 1, GROUP * CHUNK), fixed3),
        pl.BlockSpec((CONV_PAD, CONV_CH), fixed2),
        pl.BlockSpec((1, CONV_CH), fixed2),
        pl.BlockSpec((1, CONV_CH), fixed2),
        pl.BlockSpec((1, CONV_CH), fixed2),
        pl.BlockSpec((1, ATTN_WIDTH), fixed2),
        pl.BlockSpec((1, CONV_CH), fixed2),
        pl.BlockSpec((D_MODEL, D_MODEL), fixed2),
        pl.BlockSpec((1, D_MODEL), fixed2),
    ]
    args = [q, kv, kv_prev, u, u_prev, h, prm["bias"], prm["sinks"], prm["conv_w"], prm["conv_b"],
            prm["cln_g"], prm["cln_b"], prm["on_attn"], prm["on_conv"], prm["w_out"], prm["norm_ffn"]]
    out_specs = [pl.BlockSpec((tr, D_MODEL), cur), pl.BlockSpec((tr, D_MODEL), cur)]
    out_shape = [jax.ShapeDtypeStruct((m, D_MODEL), F32),
                 jax.ShapeDtypeStruct((m, D_MODEL), F32 if with_router else BF16)]
    n_units = tr // CHUNK * (N_HEADS // HEAD_PACK)
    scratch = [
        pltpu.VMEM((nb * (WINDOW + tq), 2 * KV_WIDTH), BF16),
        pltpu.VMEM((SUBLANES, nb * (CONV_PAD + tq), CONV_CH), F32),
        pltpu.VMEM((tr, ATTN_WIDTH), F32),
        pltpu.VMEM((tr, CONV_CH), F32),
        pltpu.VMEM((n_units, BAND, HEAD_PACK * CHUNK), F32),
        pltpu.VMEM((n_units, BAND, HEAD_PACK * CHUNK), BF16),
    ]
    if with_router:
        utri = (jnp.arange(tr, dtype=jnp.int32)[:, None] < jnp.arange(tr, dtype=jnp.int32)[None, :]).astype(BF16)
        in_specs += [pl.BlockSpec((D_MODEL, 2 * LANES), fixed2), pl.BlockSpec((1, LANES), fixed2),
                     pl.BlockSpec((tr, tr), fixed2)]
        args += [prm["router_w"], prm["router_b"], utri]
        out_specs += [pl.BlockSpec((tr, LANES), cur),
                      pl.BlockSpec((1, SUBLANES, tr), lambda b, t: (b * nt + t, 0, 0)),
                      pl.BlockSpec((SUBLANES, LANES), fixed2)]
        out_shape += [jax.ShapeDtypeStruct((m, LANES), F32),
                      jax.ShapeDtypeStruct((m // tr, SUBLANES, tr), F32),
                      jax.ShapeDtypeStruct((SUBLANES, LANES), F32)]
        scratch.append(pltpu.VMEM((SUBLANES, LANES), F32))
    return pl.pallas_call(
        functools.partial(_mixer_kernel, tq=tq, nb=nb, has_past=has_past, with_router=with_router),
        grid=(nseq // nb, nt),
        in_specs=in_specs,
        out_specs=out_specs,
        out_shape=out_shape,
        scratch_shapes=scratch,
        compiler_params=pltpu.CompilerParams(
            dimension_semantics=("arbitrary", "arbitrary"), vmem_limit_bytes=VMEM_LIMIT),
        name="mixer_past" if has_past else "mixer",
    )(*args)


def _ple(h2, p_ref, npl_ref, wpg_ref, wpl_ref):
    r = _rms(h2, npl_ref[...]).astype(BF16)
    gate = jax.nn.sigmoid(jnp.dot(r, wpg_ref[...], preferred_element_type=F32))
    pe = jnp.dot(p_ref[...].astype(BF16), wpl_ref[...], preferred_element_type=F32)
    return h2 + gate * pe


def _swiglu(x, wg, wu, wd):
    g = jnp.dot(x, wg, preferred_element_type=F32)
    u = jnp.dot(x, wu, preferred_element_type=F32)
    a = (g * jax.nn.sigmoid(g) * u).astype(BF16)
    return jnp.dot(a, wd, preferred_element_type=F32)


def _ffn_kernel(x_ref, h_ref, p_ref, wg_ref, wu_ref, wd_ref, npl_ref, wpg_ref, wpl_ref, o_ref):
    f = pl.program_id(1)

    @pl.when(f == 0)
    def _():
        o_ref[...] = h_ref[...]

    o_ref[...] += _swiglu(x_ref[...], wg_ref[...], wu_ref[...], wd_ref[...])

    @pl.when(f == pl.num_programs(1) - 1)
    def _():
        o_ref[...] = _ple(o_ref[...], p_ref, npl_ref, wpg_ref, wpl_ref)


def _ffn(x, h, p, layer, wg, wu, wd, npl, wpg, wpl, tm, tf):
    m = x.shape[0]
    row = lambda i, f: (i, 0)
    fixed = lambda i, f: (0, 0)
    return pl.pallas_call(
        _ffn_kernel,
        grid=(m // tm, D_FF // tf),
        in_specs=[
            pl.BlockSpec((tm, D_MODEL), row),
            pl.BlockSpec((tm, D_MODEL), row),
            pl.BlockSpec((None, tm, D_PLE), lambda i, f: (layer, i, 0)),
            pl.BlockSpec((D_MODEL, tf), lambda i, f: (0, f)),
            pl.BlockSpec((D_MODEL, tf), lambda i, f: (0, f)),
            pl.BlockSpec((tf, D_MODEL), lambda i, f: (f, 0)),
            pl.BlockSpec((1, D_MODEL), fixed),
            pl.BlockSpec((D_MODEL, D_MODEL), fixed),
            pl.BlockSpec((D_PLE, D_MODEL), fixed),
        ],
        out_specs=pl.BlockSpec((tm, D_MODEL), row),
        out_shape=jax.ShapeDtypeStruct((m, D_MODEL), F32),
        compiler_params=pltpu.CompilerParams(
            dimension_semantics=("arbitrary", "arbitrary"), vmem_limit_bytes=VMEM_LIMIT),
        name="ffn",
    )(x, h, p, wg, wu, wd, npl, wpg, wpl)


def _row_copy(src_ref, src_row, dst_ref, dst_row, sem):
    return pltpu.make_async_copy(src_ref.at[pl.ds(src_row, 1)], dst_ref.at[pl.ds(dst_row, 1)], sem)


def _dispatch_kernel(zstart_ref, zflag_ref, dest_ref, x_ref, xs_ref, zero_ref, zsem, sem, *, tm, tg, tiles):
    i = pl.program_id(0)

    @pl.when(i == 0)
    def _():
        zero_ref[...] = jnp.zeros_like(zero_ref)
        def fill(e):
            return pltpu.make_async_copy(zero_ref, xs_ref.at[pl.ds(pl.multiple_of(zstart_ref[e], tg), tg)], zsem)

        for e in range(N_FILL):
            @pl.when(zflag_ref[e] > 0)
            def _():
                fill(e).start()
        for e in range(N_FILL):
            @pl.when(zflag_ref[e] > 0)
            def _():
                fill(e).wait()

    for k in range(tiles):
        def issue(g, carry, k=k):
            base = pl.multiple_of(g * DMA_ROWS, DMA_ROWS)
            rows = x_ref.at[pl.ds(k * tm + base, DMA_ROWS)]
            for j in range(DMA_ROWS):
                for s in range(TOP_K):
                    _row_copy(rows, j, xs_ref, dest_ref[k, 0, s * tm + base + j], sem).start()
            return carry

        lax.fori_loop(0, tm // DMA_ROWS, issue, 0)

    for s in range(TOP_K):
        pltpu.make_async_copy(x_ref, xs_ref.at[pl.ds(0, tiles * tm)], sem).wait()


def _dispatch(x, dest, zstart, zflag, n_rows, tm, tg):
    m = x.shape[0]
    tiles = DISPATCH_TILES if (m // tm) % DISPATCH_TILES == 0 else 1
    return pl.pallas_call(
        functools.partial(_dispatch_kernel, tm=tm, tg=tg, tiles=tiles),
        grid_spec=pltpu.PrefetchScalarGridSpec(
            num_scalar_prefetch=2,
            grid=(m // (tiles * tm),),
            in_specs=[
                pl.BlockSpec((tiles, 1, TOP_K * tm), lambda i, zs, zf: (i, 0, 0), memory_space=pltpu.SMEM),
                pl.BlockSpec((tiles * tm, D_MODEL), lambda i, zs, zf: (i, 0)),
            ],
            out_specs=pl.BlockSpec(memory_space=pl.ANY),
            scratch_shapes=[
                pltpu.VMEM((tg, D_MODEL), F32),
                pltpu.SemaphoreType.DMA(()),
                pltpu.SemaphoreType.DMA(()),
            ],
        ),
        out_shape=jax.ShapeDtypeStruct((n_rows, D_MODEL), F32),
        compiler_params=pltpu.CompilerParams(
            dimension_semantics=("arbitrary",), vmem_limit_bytes=VMEM_LIMIT),
        name="moe_dispatch",
    )(zstart, zflag, dest, x)


def _moe_ffn_kernel(te_ref, nused_ref, x_ref, wg_ref, wu_ref, wd_ref, o_ref):
    i = pl.program_id(0)
    f = pl.program_id(1)

    @pl.when((i >= nused_ref[0]) & (f == 0))
    def _():
        o_ref[...] = jnp.zeros_like(o_ref)

    @pl.when(i < nused_ref[0])
    def _():
        y = _swiglu(x_ref[...].astype(BF16), wg_ref[0], wu_ref[0], wd_ref[0])

        @pl.when(f == 0)
        def _():
            o_ref[...] = y

        @pl.when(f > 0)
        def _():
            o_ref[...] += y


def _moe_ffn(xs, te, nused, wg, wu, wd, tg, tf):
    n_rows = xs.shape[0]
    nf = D_FF // tf
    tile = lambda i, nu: jnp.minimum(i, nu[0] - 1)
    fstep = lambda i, f, nu: jnp.where(i < nu[0], f, nf - 1)
    return pl.pallas_call(
        _moe_ffn_kernel,
        grid_spec=pltpu.PrefetchScalarGridSpec(
            num_scalar_prefetch=2,
            grid=(n_rows // tg, nf),
            in_specs=[
                pl.BlockSpec((tg, D_MODEL), lambda i, f, te, nu: (tile(i, nu), 0)),
                pl.BlockSpec((1, D_MODEL, tf), lambda i, f, te, nu: (te[tile(i, nu)], 0, fstep(i, f, nu))),
                pl.BlockSpec((1, D_MODEL, tf), lambda i, f, te, nu: (te[tile(i, nu)], 0, fstep(i, f, nu))),
                pl.BlockSpec((1, tf, D_MODEL), lambda i, f, te, nu: (te[tile(i, nu)], fstep(i, f, nu), 0)),
            ],
            out_specs=pl.BlockSpec((tg, D_MODEL), lambda i, f, te, nu: (i, 0)),
        ),
        out_shape=jax.ShapeDtypeStruct((n_rows, D_MODEL), F32),
        compiler_params=pltpu.CompilerParams(
            dimension_semantics=("arbitrary", "arbitrary"), vmem_limit_bytes=VMEM_LIMIT),
        name="moe_ffn",
    )(te, nused, xs, wg, wu, wd)


def _combine_kernel(dest_ref, dest_next_ref, route_ref, h_ref, p_ref, npl_ref, wpg_ref, wpl_ref, ys_ref, o_ref,
                    buf_ref, sem, *, tm):
    i = pl.program_id(0)
    cur = i % 2

    def issue_rows(rows_ref, half, base):
        for s in range(TOP_K):
            rows = buf_ref.at[half, s, pl.ds(base, DMA_ROWS)]
            for j in range(DMA_ROWS):
                _row_copy(ys_ref, rows_ref[0, 0, s * tm + base + j], rows, j, sem.at[half]).start()

    def wait_half(half):
        for s in range(TOP_K):
            pltpu.make_async_copy(ys_ref.at[pl.ds(0, tm)], buf_ref.at[half, s], sem.at[half]).wait()

    @pl.when(i == 0)
    def _():
        def issue(g, carry):
            issue_rows(dest_ref, 0, pl.multiple_of(g * DMA_ROWS, DMA_ROWS))
            return carry

        lax.fori_loop(0, tm // DMA_ROWS, issue, 0)

    wait_half(cur)

    g1 = route_ref[:, 2:3]
    g2 = route_ref[:, 3:4]
    h2 = h_ref[...] + (g1 * buf_ref[cur, 0] + g2 * buf_ref[cur, 1])

    for g in range(tm // DMA_ROWS):
        issue_rows(dest_next_ref, 1 - cur, g * DMA_ROWS)

    o_ref[...] = _ple(h2, p_ref, npl_ref, wpg_ref, wpl_ref)

    @pl.when(i == pl.num_programs(0) - 1)
    def _():
        wait_half(1 - cur)


def _combine(ys, dest, route, h, p, layer, npl, wpg, wpl, tm):
    m = h.shape[0]
    row = lambda i: (i, 0)
    fixed = lambda i: (0, 0)
    return pl.pallas_call(
        functools.partial(_combine_kernel, tm=tm),
        grid=(m // tm,),
        in_specs=[
            pl.BlockSpec((1, 1, TOP_K * tm), lambda i: (i, 0, 0), memory_space=pltpu.SMEM),
            pl.BlockSpec((1, 1, TOP_K * tm), lambda i: (jnp.minimum(i + 1, m // tm - 1), 0, 0),
                         memory_space=pltpu.SMEM),
            pl.BlockSpec((tm, LANES), row),
            pl.BlockSpec((tm, D_MODEL), row),
            pl.BlockSpec((None, tm, D_PLE), lambda i: (layer, i, 0)),
            pl.BlockSpec((1, D_MODEL), fixed),
            pl.BlockSpec((D_MODEL, D_MODEL), fixed),
            pl.BlockSpec((D_PLE, D_MODEL), fixed),
            pl.BlockSpec(memory_space=pl.ANY),
        ],
        out_specs=pl.BlockSpec((tm, D_MODEL), row),
        out_shape=jax.ShapeDtypeStruct((m, D_MODEL), F32),
        scratch_shapes=[
            pltpu.VMEM((2, TOP_K, tm, D_MODEL), F32),
            pltpu.SemaphoreType.DMA((2,)),
        ],
        compiler_params=pltpu.CompilerParams(
            dimension_semantics=("arbitrary",), vmem_limit_bytes=VMEM_LIMIT),
        name="moe_combine",
    )(dest, dest, route, h, p, npl, wpg, wpl, ys)


def _routing(route_t, count, m, tm, tg):
    rows = jnp.transpose(route_t, (1, 0, 2)).reshape(SUBLANES, m)
    counts = count[:N_EXPERTS, 0].astype(jnp.int32)
    padded = (counts + tg - 1) // tg * tg
    ends = jnp.cumsum(padded)
    offs = ends - padded

    def dest_rows(expert, rank):
        expert = expert.astype(jnp.int32)
        start = sum(jnp.where(expert == k, offs[k], 0) for k in range(N_EXPERTS))
        return (start + rank.astype(jnp.int32)).reshape(m // tm, 1, tm)

    dest = jnp.concatenate([dest_rows(rows[s], rows[2 * TOP_K + s]) for s in range(TOP_K)], axis=2)
    n_tiles = TOP_K * m // tg + N_EXPERTS
    nused = ends[-1] // tg
    tile_ids = jnp.minimum(jnp.arange(n_tiles, dtype=jnp.int32), nused - 1)
    te = jnp.sum((tile_ids[:, None] >= (ends // tg)[None, :]).astype(jnp.int32), axis=1)
    te = jnp.minimum(te, N_EXPERTS - 1)
    slack = nused + jnp.arange(N_FILL - N_EXPERTS, dtype=jnp.int32)
    zstart = jnp.concatenate([jnp.maximum(ends - tg, 0), jnp.minimum(slack, n_tiles - 1) * tg])
    zflag = jnp.concatenate([counts > 0, slack < n_tiles]).astype(jnp.int32)
    return (dest, te.astype(jnp.int32), nused.reshape(1).astype(jnp.int32),
            zstart.astype(jnp.int32), zflag, n_tiles * tg)


def _moe(hn, h1, p, layer, route, route_t, count, wg, wu, wd, npl, wpg, wpl, tm, tg, tf):
    dest, te, nused, zstart, zflag, n_rows = _routing(route_t, count, hn.shape[0], tm, tg)
    xs = _dispatch(hn, dest, zstart, zflag, n_rows, tm, tg)
    ys = _moe_ffn(xs, te, nused, wg, wu, wd, tg, tf)
    return _combine(ys, dest, route, h1, p, layer, npl, wpg, wpl, tm)


def _alibi_bias():
    slopes = jnp.exp2(-8.0 * jnp.arange(1, N_HEADS + 1, dtype=F32) / N_HEADS)
    qi = jnp.arange(CHUNK, dtype=jnp.int32)[None, :]
    kj = jnp.arange(BAND, dtype=jnp.int32)[:, None]
    dist = jnp.abs(qi + WINDOW - kj).astype(F32)
    bias = slopes[:, None, None] * dist[None]
    bias = bias.reshape(N_KV_HEADS, GROUP, BAND, CHUNK)
    return jnp.transpose(bias, (0, 2, 1, 3)).reshape(N_KV_HEADS, BAND, GROUP * CHUNK)


def _head_sum_matrix():
    head = jnp.arange(QK_WIDTH, dtype=jnp.int32) // HEAD_DIM
    return (head[:, None] == head[None, :]).astype(BF16)


def _layer_params(l, norm_mix, w_in, q_gain, k_gain, attn_sinks, conv_w, conv_b, conv_ln_g, conv_ln_b,
                  out_norm_attn, out_norm_conv, w_out, norm_ffn, router_w, router_b, ple_norm,
                  w_ple_gate, w_ple):
    scale = HEAD_DIM ** -0.5
    prm = {
        "norm_mix": norm_mix[l][None, :],
        "w_in": w_in[l].astype(BF16),
        "qk_gain": jnp.concatenate([jnp.tile(q_gain[l] * scale, N_HEADS), jnp.tile(k_gain[l], N_KV_HEADS)])[None, :],
        "sinks": jnp.broadcast_to(attn_sinks[l].reshape(N_KV_HEADS, 1, GROUP, 1),
                                  (N_KV_HEADS, 1, GROUP, CHUNK)).reshape(N_KV_HEADS, 1, GROUP * CHUNK),
        "conv_w": jnp.pad(conv_w[l], ((0, CONV_PAD - CONV_K), (0, 0))),
        "conv_b": conv_b[l][None, :],
        "cln_g": conv_ln_g[l][None, :],
        "cln_b": conv_ln_b[l][None, :],
        "on_attn": out_norm_attn[l][None, :],
        "on_conv": out_norm_conv[l][None, :],
        "w_out": w_out[l].astype(BF16),
        "norm_ffn": norm_ffn[l][None, :],
        "ple_norm": ple_norm[l][None, :],
        "w_ple_gate": w_ple_gate[l].astype(BF16),
        "w_ple": w_ple[l].astype(BF16),
        "router_w": None,
        "router_b": None,
    }
    if l % 2 == 1:
        i = l // 2
        rw = jnp.pad(router_w[i], ((0, 0), (0, LANES - N_EXPERTS)))
        rw_hi = rw.astype(BF16)
        rw_lo = (rw - rw_hi.astype(F32)).astype(BF16)
        prm["router_w"] = jnp.concatenate([rw_hi, rw_lo], axis=1)
        prm["router_b"] = jnp.pad(router_b[i], (0, LANES - N_EXPERTS))[None, :]
    return prm


def _trunk(x, p, cache_k, cache_v, state_conv, layers, ffn_w, bias, hsum, *, tm, tq, nb=1):
    nseq, seq, _ = x.shape
    m = nseq * seq
    has_past = cache_k is not None
    h = x.reshape(m, D_MODEL)
    p2 = p.reshape(p.shape[0], m, D_PLE)
    win_k, win_v, convs = [], [], []
    for l, prm in enumerate(layers):
        q, kv, u = _in_proj(h, prm["norm_mix"], prm["w_in"], prm["qk_gain"], hsum, tm)
        if has_past:
            win = cache_k.shape[2]
            kv_prev = jnp.concatenate([cache_k[l].reshape(nseq * win, KV_WIDTH),
                                       cache_v[l].reshape(nseq * win, KV_WIDTH)], axis=1)
            u_prev = jnp.pad(state_conv[l], ((0, 0), (CONV_PAD - (CONV_K - 1), 0), (0, 0))).reshape(
                nseq * CONV_PAD, CONV_CH)
        else:
            kv_prev, u_prev = kv, u
        outs = _mixer(q, kv, kv_prev, u, u_prev, h, dict(prm, bias=bias), nseq=nseq, tq=tq, nb=nb, has_past=has_past)
        h1, hn = outs[0], outs[1]
        wg, wu, wd = ffn_w[l]
        if len(outs) > 2:
            route, route_t, count = outs[2:]
            h = _moe(hn, h1, p2, l, route, route_t, count, wg, wu, wd,
                     prm["ple_norm"], prm["w_ple_gate"], prm["w_ple"], tm, tm, D_FF // 2)
        else:
            h = _ffn(hn, h1, p2, l, wg, wu, wd, prm["ple_norm"], prm["w_ple_gate"], prm["w_ple"], tm, D_FF // 2)
        kv3 = kv.reshape(nseq, seq, 2 * KV_WIDTH)
        u3 = u.reshape(nseq, seq, CONV_CH)
        if has_past:
            kv3 = jnp.concatenate([kv_prev.reshape(nseq, win, 2 * KV_WIDTH), kv3], axis=1)[:, -win:]
            u3 = jnp.concatenate([state_conv[l], u3], axis=1)
        else:
            kv3 = kv3[:, seq - WINDOW:]
        win_k.append(kv3[..., :KV_WIDTH].reshape(nseq, -1, N_KV_HEADS, HEAD_DIM))
        win_v.append(kv3[..., KV_WIDTH:].reshape(nseq, -1, N_KV_HEADS, HEAD_DIM))
        convs.append(u3[:, -(CONV_K - 1):])
    return h.reshape(nseq, seq, D_MODEL), jnp.stack(win_k), jnp.stack(win_v), jnp.stack(convs)


def kernel(x_prompt, x_sample, p_prompt, p_sample, cache_k, cache_v, state_conv, norm_mix, w_in, q_gain, k_gain, attn_sinks, conv_w, conv_b, conv_ln_g, conv_ln_b, out_norm_attn, out_norm_conv, w_out, norm_ffn, ffn_gate, ffn_up, ffn_down, router_w, router_b, moe_gate, moe_up, moe_down, ple_norm, w_ple_gate, w_ple):
    depth = w_in.shape[0]
    layers = [
        _layer_params(l, norm_mix, w_in, q_gain, k_gain, attn_sinks, conv_w, conv_b, conv_ln_g, conv_ln_b,
                      out_norm_attn, out_norm_conv, w_out, norm_ffn, router_w, router_b, ple_norm,
                      w_ple_gate, w_ple)
        for l in range(depth)
    ]
    ffn_w = []
    for l in range(depth):
        i = l // 2
        if l % 2 == 0:
            ffn_w.append((ffn_gate[i].astype(BF16), ffn_up[i].astype(BF16), ffn_down[i].astype(BF16)))
        else:
            ffn_w.append((moe_gate[i].astype(BF16), moe_up[i].astype(BF16), moe_down[i].astype(BF16)))
    bias = _alibi_bias()
    hsum = _head_sum_matrix()
    tm_p = min(512, x_prompt.shape[0] * x_prompt.shape[1])
    tq_p = min(512, x_prompt.shape[1])
    y_p, wk_p, wv_p, cv_p = _trunk(x_prompt, p_prompt, None, None, None, layers, ffn_w, bias, hsum,
                                   tm=tm_p, tq=tq_p)
    tm_s = min(512, x_sample.shape[0] * x_sample.shape[1])
    nb_s = max(d for d in range(1, x_sample.shape[0] + 1)
               if x_sample.shape[0] % d == 0 and d * x_sample.shape[1] <= tm_s)
    y_s, wk_s, wv_s, cv_s = _trunk(x_sample, p_sample, cache_k, cache_v, state_conv, layers, ffn_w, bias, hsum,
                                   tm=tm_s, tq=x_sample.shape[1], nb=nb_s)
    return (y_p, y_s, wk_p, wv_p, cv_p, wk_s, wv_s, cv_s)
```

```python
import functools

import jax
import jax.numpy as jnp
from jax import lax
from jax.experimental import pallas as pl
from jax.experimental.pallas import tpu as pltpu

D_MODEL = 1024
CHUNK = 64
WINDOW = 128
N_HEADS = 8
N_KV_HEADS = 2
HEAD_DIM = 64
GROUP = N_HEADS // N_KV_HEADS
HEAD_PACK = 4
ATTN_WIDTH = N_HEADS * HEAD_DIM
KV_WIDTH = N_KV_HEADS * HEAD_DIM
QK_WIDTH = ATTN_WIDTH + KV_WIDTH
CONV_CH = 512
CONV_K = 31
SUBLANES = 8
CONV_PAD = 32
CONV_ROWS = 64
CONV_ACCS = 2
D_IN = ATTN_WIDTH + 2 * KV_WIDTH + 2 * CONV_CH
BAND = WINDOW + CHUNK
D_FF = 2816
N_EXPERTS = 8
TOP_K = 2
N_FILL = 2 * N_EXPERTS
D_PLE = 256
DMA_ROWS = 8
DISPATCH_TILES = 2
EPS = 1e-6
LANES = 128

F32 = jnp.float32
BF16 = jnp.bfloat16

VMEM_LIMIT = 56 * 1024 * 1024

def _rms(x, g):
    return x * lax.rsqrt(jnp.mean(x * x, axis=-1, keepdims=True) + EPS) * g


def _in_proj_kernel(h_ref, nrm_ref, w_ref, gain_ref, hsum_ref, q_ref, kv_ref, u_ref):
    xn = _rms(h_ref[...], nrm_ref[...])
    z = jnp.dot(xn.astype(BF16), w_ref[...], preferred_element_type=F32)
    qk = z[:, :QK_WIDTH]
    ss = jnp.dot((qk * qk).astype(BF16), hsum_ref[...], preferred_element_type=F32)
    qkn = qk * lax.rsqrt(ss * (1.0 / HEAD_DIM) + EPS) * gain_ref[...]
    q_ref[...] = qkn[:, :ATTN_WIDTH].astype(q_ref.dtype)
    kv_ref[:, :KV_WIDTH] = qkn[:, ATTN_WIDTH:]
    kv_ref[:, KV_WIDTH:] = z[:, QK_WIDTH:QK_WIDTH + KV_WIDTH]
    a = z[:, QK_WIDTH + KV_WIDTH:QK_WIDTH + KV_WIDTH + CONV_CH]
    gl = z[:, QK_WIDTH + KV_WIDTH + CONV_CH:]
    u_ref[...] = a * jax.nn.sigmoid(gl)


def _in_proj(h, nrm, w_in, qk_gain, hsum, tm):
    m = h.shape[0]
    row = lambda i: (i, 0)
    fixed = lambda i: (0, 0)
    return pl.pallas_call(
        _in_proj_kernel,
        grid=(m // tm,),
        in_specs=[
            pl.BlockSpec((tm, D_MODEL), row),
            pl.BlockSpec((1, D_MODEL), fixed),
            pl.BlockSpec((D_MODEL, D_IN), fixed),
            pl.BlockSpec((1, QK_WIDTH), fixed),
            pl.BlockSpec((QK_WIDTH, QK_WIDTH), fixed),
        ],
        out_specs=[
            pl.BlockSpec((tm, ATTN_WIDTH), row),
            pl.BlockSpec((tm, 2 * KV_WIDTH), row),
            pl.BlockSpec((tm, CONV_CH), row),
        ],
        out_shape=[
            jax.ShapeDtypeStruct((m, ATTN_WIDTH), BF16),
            jax.ShapeDtypeStruct((m, 2 * KV_WIDTH), F32),
            jax.ShapeDtypeStruct((m, CONV_CH), F32),
        ],
        compiler_params=pltpu.CompilerParams(
            dimension_semantics=("arbitrary",), vmem_limit_bytes=VMEM_LIMIT),
        name="in_proj",
    )(h, nrm, w_in, qk_gain, hsum)


def _mixer_kernel(q_ref, kv_ref, kvp_ref, u_ref, up_ref, h_ref, bias_ref, sink_ref,
                  cw_ref, cb_ref, lng_ref, lnb_ref, ona_ref, onc_ref, wout_ref, nffn_ref,
                  *rest, tq, nb, has_past, with_router):
    if with_router:
        (rw_ref, rb_ref, utri_ref, h1_ref, hn_ref, route_ref, routet_ref, count_ref,
         kvx_ref, ux_ref, oa_ref, cv_ref, s_ref, p_ref, cnt_ref) = rest
    else:
        h1_ref, hn_ref, kvx_ref, ux_ref, oa_ref, cv_ref, s_ref, p_ref = rest
    t = pl.program_id(1)
    nch = tq // CHUNK
    kv_seg = WINDOW + tq
    u_seg = CONV_PAD + tq

    for i in range(nb):
        kvx_ref[i * kv_seg:i * kv_seg + WINDOW, :] = kvp_ref[i * WINDOW:(i + 1) * WINDOW, :].astype(BF16)
        kvx_ref[i * kv_seg + WINDOW:(i + 1) * kv_seg, :] = kv_ref[i * tq:(i + 1) * tq, :].astype(BF16)

    packs = N_HEADS // HEAD_PACK
    units = [(i * tq + c * CHUNK, i * kv_seg + c * CHUNK, c, hp)
             for i in range(nb) for c in range(nch) for hp in range(packs)]

    def unit_operands(hp):
        j = hp * HEAD_PACK // GROUP
        lanes = slice((hp * HEAD_PACK % GROUP) * CHUNK, (hp * HEAD_PACK % GROUP + HEAD_PACK) * CHUNK)
        return j, lanes

    for u, (qrow, krow, c, hp) in enumerate(units):
        j, lanes = unit_operands(hp)
        qc = q_ref[qrow:qrow + CHUNK, :]
        qs = jnp.concatenate([qc[:, h * HEAD_DIM:(h + 1) * HEAD_DIM]
                              for h in range(hp * HEAD_PACK, (hp + 1) * HEAD_PACK)],
                             axis=0)
        kj = kvx_ref[krow:krow + BAND, j * HEAD_DIM:(j + 1) * HEAD_DIM]
        s = lax.dot_general(kj, qs, (((1,), (1,)), ((), ())), preferred_element_type=F32)
        s = s - bias_ref[j, :, lanes]
        if not has_past and c < WINDOW // CHUNK:
            kpos = lax.broadcasted_iota(jnp.int32, (BAND, 1), 0) + (t * tq + c * CHUNK - WINDOW)
            s = jnp.where(kpos >= 0, s, -jnp.inf)
        s_ref[u] = s

    for u, (qrow, krow, c, hp) in enumerate(units):
        j, lanes = unit_operands(hp)
        s = s_ref[u]
        sink = sink_ref[j, :, lanes]
        mx = jnp.maximum(jnp.max(s, axis=0, keepdims=True), sink)
        e = jnp.exp(s - mx)
        den = jnp.sum(e, axis=0, keepdims=True) + jnp.exp(sink - mx)
        p_ref[u] = (e * (1.0 / den)).astype(BF16)

    for u0 in range(0, len(units), packs):
        outs = []
        for u in range(u0, u0 + packs):
            qrow, krow, c, hp = units[u]
            j, _ = unit_operands(hp)
            vj = kvx_ref[krow:krow + BAND, KV_WIDTH + j * HEAD_DIM:KV_WIDTH + (j + 1) * HEAD_DIM]
            o = lax.dot_general(p_ref[u], vj, (((0,), (0,)), ((), ())),
                                preferred_element_type=F32)
            outs.extend(o[g * CHUNK:(g + 1) * CHUNK, :] for g in range(HEAD_PACK))
        oa_ref[qrow:qrow + CHUNK, :] = jnp.concatenate(outs, axis=1)

    for i in range(nb):
        prefix = up_ref[i * CONV_PAD:(i + 1) * CONV_PAD, :]
        ux_ref[0, i * u_seg:i * u_seg + CONV_PAD, :] = prefix if has_past else jnp.where(t > 0, prefix, 0.0)
        ux_ref[0, i * u_seg + CONV_PAD:(i + 1) * u_seg, :] = u_ref[i * tq:(i + 1) * tq, :]
    shifted_rows = nb * u_seg - SUBLANES
    for r in range(1, SUBLANES):
        ux_ref[r, :shifted_rows, :] = ux_ref[0, r:r + shifted_rows, :]
    lead = CONV_PAD - (CONV_K - 1)

    for cg in range(CONV_CH // LANES):
        lanes = slice(cg * LANES, (cg + 1) * LANES)
        taps = [cw_ref[k:k + 1, lanes] for k in range(CONV_K)]

        def conv_rows(i, carry, lanes=lanes, taps=taps):
            out0 = pl.multiple_of(i * CONV_ROWS, CONV_ROWS)
            r0 = out0 if nb == 1 else pl.multiple_of(i * u_seg, SUBLANES)
            accs = [None] * CONV_ACCS
            for r in range(SUBLANES):
                steps = [(k, (lead + k) // SUBLANES) for k in range(CONV_K) if (lead + k) % SUBLANES == r]
                slab = ux_ref[r, pl.ds(r0, CONV_ROWS + steps[-1][1] * SUBLANES), lanes]
                for k, a in steps:
                    term = slab[a * SUBLANES:a * SUBLANES + CONV_ROWS, :] * taps[k]
                    accs[k % CONV_ACCS] = term if accs[k % CONV_ACCS] is None else accs[k % CONV_ACCS] + term
            cv_ref[pl.ds(out0, CONV_ROWS), lanes] = functools.reduce(lambda x, y: x + y, accs) + cb_ref[:, lanes]
            return carry

        lax.fori_loop(0, nb * tq // CONV_ROWS, conv_rows, 0)

    cv = cv_ref[...]
    mu = jnp.mean(cv, axis=-1, keepdims=True)
    xc = cv - mu
    ln = xc * lax.rsqrt(jnp.mean(xc * xc, axis=-1, keepdims=True) + EPS) * lng_ref[...] + lnb_ref[...]
    oc = ln * jax.nn.sigmoid(ln)
    cn = _rms(oc, onc_ref[...]).astype(BF16)

    an = _rms(oa_ref[...], ona_ref[...]).astype(BF16)
    mixed = (jnp.dot(an, wout_ref[:ATTN_WIDTH, :], preferred_element_type=F32)
             + jnp.dot(cn, wout_ref[ATTN_WIDTH:, :], preferred_element_type=F32))
    h1 = h_ref[...] + mixed
    h1_ref[...] = h1
    hn = _rms(h1, nffn_ref[...])
    hn_ref[...] = hn.astype(hn_ref.dtype)

    if with_router:
        hi = hn.astype(BF16)
        lo = (hn - hi.astype(F32)).astype(BF16)
        part = jnp.dot(hi, rw_ref[...], preferred_element_type=F32)
        logits = (part[:, :LANES] + part[:, LANES:]
                  + jnp.dot(lo, rw_ref[:, :LANES], preferred_element_type=F32) + rb_ref[...])
        lt = logits.T[:N_EXPERTS, :]
        row = lax.broadcasted_iota(jnp.int32, lt.shape, 0)
        m1 = jnp.max(lt, axis=0, keepdims=True)
        i1 = jnp.min(jnp.where(lt == m1, row, N_EXPERTS), axis=0, keepdims=True)
        rest_l = jnp.where(row == i1, -jnp.inf, lt)
        m2 = jnp.max(rest_l, axis=0, keepdims=True)
        i2 = jnp.min(jnp.where(rest_l == m2, row, N_EXPERTS), axis=0, keepdims=True)
        e2 = jnp.exp(m2 - m1)
        den = 1.0 + e2

        @pl.when((pl.program_id(0) == 0) & (t == 0))
        def _():
            cnt_ref[...] = jnp.zeros_like(cnt_ref)

        sel = ((row == i1) | (row == i2)).astype(F32)
        ahead = jnp.dot(sel.astype(BF16), utri_ref[...], preferred_element_type=F32) + cnt_ref[:, 0:1]
        r1 = jnp.sum(jnp.where(row == i1, ahead, 0.0), axis=0, keepdims=True)
        r2 = jnp.sum(jnp.where(row == i2, ahead, 0.0), axis=0, keepdims=True)
        cnt_ref[...] += jnp.sum(sel, axis=1, keepdims=True)
        count_ref[...] = cnt_ref[...]

        route_t = jnp.concatenate(
            [i1.astype(F32), i2.astype(F32), 1.0 / den, e2 / den, r1, r2,
             jnp.zeros((SUBLANES - 3 * TOP_K, nb * tq), F32)], axis=0)
        routet_ref[0] = route_t
        route_ref[...] = jnp.concatenate([route_t, jnp.zeros((LANES - SUBLANES, nb * tq), F32)], axis=0).T


def _mixer(q, kv, kv_prev, u, u_prev, h, prm, *, nseq, tq, nb, has_past):
    m = h.shape[0]
    seq = m // nseq
    nt = seq // tq
    assert nb == 1 or (has_past and nt == 1 and tq == CONV_ROWS and nseq % nb == 0)
    tr = nb * tq
    with_router = prm["router_w"] is not None
    cur = lambda b, t: (b * nt + t, 0)
    fixed2 = lambda b, t: (0, 0)
    fixed3 = lambda b, t: (0, 0, 0)
    if has_past:
        kvp_map = lambda b, t: (b, 0)
        up_map = lambda b, t: (b, 0)
    else:
        kvp_map = lambda b, t: (jnp.maximum(b * (seq // WINDOW) + t * (tq // WINDOW) - 1, 0), 0)
        up_map = lambda b, t: (jnp.maximum(b * (seq // CONV_PAD) + t * (tq // CONV_PAD) - 1, 0), 0)
    in_specs = [
        pl.BlockSpec((tr, ATTN_WIDTH), cur),
        pl.BlockSpec((tr, 2 * KV_WIDTH), cur),
        pl.BlockSpec((nb * WINDOW, 2 * KV_WIDTH), kvp_map),
        pl.BlockSpec((tr, CONV_CH), cur),
        pl.BlockSpec((nb * CONV_PAD, CONV_CH), up_map),
        pl.BlockSpec((tr, D_MODEL), cur),
        pl.BlockSpec((N_KV_HEADS, BAND, GROUP * CHUNK), fixed3),
        pl.BlockSpec((N_KV_HEADS, 1, GROUP * CHUNK), fixed3),
        pl.BlockSpec((CONV_PAD, CONV_CH), fixed2),
        pl.BlockSpec((1, CONV_CH), fixed2),
        pl.BlockSpec((1, CONV_CH), fixed2),
        pl.BlockSpec((1, CONV_CH), fixed2),
        pl.BlockSpec((1, ATTN_WIDTH), fixed2),
        pl.BlockSpec((1, CONV_CH), fixed2),
        pl.BlockSpec((D_MODEL, D_MODEL), fixed2),
        pl.BlockSpec((1, D_MODEL), fixed2),
    ]
    args = [q, kv, kv_prev, u, u_prev, h, prm["bias"], prm["sinks"], prm["conv_w"], prm["conv_b"],
            prm["cln_g"], prm["cln_b"], prm["on_attn"], prm["on_conv"], prm["w_out"], prm["norm_ffn"]]
    out_specs = [pl.BlockSpec((tr, D_MODEL), cur), pl.BlockSpec((tr, D_MODEL), cur)]
    out_shape = [jax.ShapeDtypeStruct((m, D_MODEL), F32),
                 jax.ShapeDtypeStruct((m, D_MODEL), F32 if with_router else BF16)]
    n_units = tr // CHUNK * (N_HEADS // HEAD_PACK)
    scratch = [
        pltpu.VMEM((nb * (WINDOW + tq), 2 * KV_WIDTH), BF16),
        pltpu.VMEM((SUBLANES, nb * (CONV_PAD + tq), CONV_CH), F32),
        pltpu.VMEM((tr, ATTN_WIDTH), F32),
        pltpu.VMEM((tr, CONV_CH), F32),
        pltpu.VMEM((n_units, BAND, HEAD_PACK * CHUNK), F32),
        pltpu.VMEM((n_units, BAND, HEAD_PACK * CHUNK), BF16),
    ]
    if with_router:
        utri = (jnp.arange(tr, dtype=jnp.int32)[:, None] < jnp.arange(tr, dtype=jnp.int32)[None, :]).astype(BF16)
        in_specs += [pl.BlockSpec((D_MODEL, 2 * LANES), fixed2), pl.BlockSpec((1, LANES), fixed2),
                     pl.BlockSpec((tr, tr), fixed2)]
        args += [prm["router_w"], prm["router_b"], utri]
        out_specs += [pl.BlockSpec((tr, LANES), cur),
                      pl.BlockSpec((1, SUBLANES, tr), lambda b, t: (b * nt + t, 0, 0)),
                      pl.BlockSpec((SUBLANES, LANES), fixed2)]
        out_shape += [jax.ShapeDtypeStruct((m, LANES), F32),
                      jax.ShapeDtypeStruct((m // tr, SUBLANES, tr), F32),
                      jax.ShapeDtypeStruct((SUBLANES, LANES), F32)]
        scratch.append(pltpu.VMEM((SUBLANES, LANES), F32))
    return pl.pallas_call(
        functools.partial(_mixer_kernel, tq=tq, nb=nb, has_past=has_past, with_router=with_router),
        grid=(nseq // nb, nt),
        in_specs=in_specs,
        out_specs=out_specs,
        out_shape=out_shape,
        scratch_shapes=scratch,
        compiler_params=pltpu.CompilerParams(
            dimension_semantics=("arbitrary", "arbitrary"), vmem_limit_bytes=VMEM_LIMIT),
        name="mixer_past" if has_past else "mixer",
    )(*args)


def _ple(h2, p_ref, npl_ref, wpg_ref, wpl_ref):
    r = _rms(h2, npl_ref[...]).astype(BF16)
    gate = jax.nn.sigmoid(jnp.dot(r, wpg_ref[...], preferred_element_type=F32))
    pe = jnp.dot(p_ref[...].astype(BF16), wpl_ref[...], preferred_element_type=F32)
    return h2 + gate * pe


def _swiglu(x, wg, wu, wd):
    g = jnp.dot(x, wg, preferred_element_type=F32)
    u = jnp.dot(x, wu, preferred_element_type=F32)
    a = (g * jax.nn.sigmoid(g) * u).astype(BF16)
    return jnp.dot(a, wd, preferred_element_type=F32)


def _ffn_kernel(x_ref, h_ref, p_ref, wg_ref, wu_ref, wd_ref, npl_ref, wpg_ref, wpl_ref, o_ref):
    f = pl.program_id(1)

    @pl.when(f == 0)
    def _():
        o_ref[...] = h_ref[...]

    o_ref[...] += _swiglu(x_ref[...], wg_ref[...], wu_ref[...], wd_ref[...])

    @pl.when(f == pl.num_programs(1) - 1)
    def _():
        o_ref[...] = _ple(o_ref[...], p_ref, npl_ref, wpg_ref, wpl_ref)


def _ffn(x, h, p, layer, wg, wu, wd, npl, wpg, wpl, tm, tf):
    m = x.shape[0]
    row = lambda i, f: (i, 0)
    fixed = lambda i, f: (0, 0)
    weight_mode = pl.Buffered(1) if tf == D_FF else None
    return pl.pallas_call(
        _ffn_kernel,
        grid=(m // tm, D_FF // tf),
        in_specs=[
            pl.BlockSpec((tm, D_MODEL), row),
            pl.BlockSpec((tm, D_MODEL), row),
            pl.BlockSpec((None, tm, D_PLE), lambda i, f: (layer, i, 0)),
            pl.BlockSpec((D_MODEL, tf), lambda i, f: (0, f), pipeline_mode=weight_mode),
            pl.BlockSpec((D_MODEL, tf), lambda i, f: (0, f), pipeline_mode=weight_mode),
            pl.BlockSpec((tf, D_MODEL), lambda i, f: (f, 0), pipeline_mode=weight_mode),
            pl.BlockSpec((1, D_MODEL), fixed),
            pl.BlockSpec((D_MODEL, D_MODEL), fixed),
            pl.BlockSpec((D_PLE, D_MODEL), fixed),
        ],
        out_specs=pl.BlockSpec((tm, D_MODEL), row),
        out_shape=jax.ShapeDtypeStruct((m, D_MODEL), F32),
        compiler_params=pltpu.CompilerParams(
            dimension_semantics=("arbitrary", "arbitrary"), vmem_limit_bytes=VMEM_LIMIT),
        name="ffn",
    )(x, h, p, wg, wu, wd, npl, wpg, wpl)


def _row_copy(src_ref, src_row, dst_ref, dst_row, sem):
    return pltpu.make_async_copy(src_ref.at[pl.ds(src_row, 1)], dst_ref.at[pl.ds(dst_row, 1)], sem)


def _dispatch_kernel(zstart_ref, zflag_ref, dest_ref, x_ref, xs_ref, zero_ref, zsem, sem, *, tm, tg, tiles):
    i = pl.program_id(0)

    @pl.when(i == 0)
    def _():
        zero_ref[...] = jnp.zeros_like(zero_ref)
        def fill(e):
            return pltpu.make_async_copy(zero_ref, xs_ref.at[pl.ds(pl.multiple_of(zstart_ref[e], tg), tg)], zsem)

        for e in range(N_FILL):
            @pl.when(zflag_ref[e] > 0)
            def _():
                fill(e).start()
        for e in range(N_FILL):
            @pl.when(zflag_ref[e] > 0)
            def _():
                fill(e).wait()

    for k in range(tiles):
        def issue(g, carry, k=k):
            base = pl.multiple_of(g * DMA_ROWS, DMA_ROWS)
            rows = x_ref.at[pl.ds(k * tm + base, DMA_ROWS)]
            for j in range(DMA_ROWS):
                for s in range(TOP_K):
                    _row_copy(rows, j, xs_ref, dest_ref[k, 0, s * tm + base + j], sem).start()
            return carry

        lax.fori_loop(0, tm // DMA_ROWS, issue, 0)

    for s in range(TOP_K):
        pltpu.make_async_copy(x_ref, xs_ref.at[pl.ds(0, tiles * tm)], sem).wait()


def _dispatch(x, dest, zstart, zflag, n_rows, tm, tg):
    m = x.shape[0]
    tiles = DISPATCH_TILES if (m // tm) % DISPATCH_TILES == 0 else 1
    return pl.pallas_call(
        functools.partial(_dispatch_kernel, tm=tm, tg=tg, tiles=tiles),
        grid_spec=pltpu.PrefetchScalarGridSpec(
            num_scalar_prefetch=2,
            grid=(m // (tiles * tm),),
            in_specs=[
                pl.BlockSpec((tiles, 1, TOP_K * tm), lambda i, zs, zf: (i, 0, 0), memory_space=pltpu.SMEM),
                pl.BlockSpec((tiles * tm, D_MODEL), lambda i, zs, zf: (i, 0)),
            ],
            out_specs=pl.BlockSpec(memory_space=pl.ANY),
            scratch_shapes=[
                pltpu.VMEM((tg, D_MODEL), F32),
                pltpu.SemaphoreType.DMA(()),
                pltpu.SemaphoreType.DMA(()),
            ],
        ),
        out_shape=jax.ShapeDtypeStruct((n_rows, D_MODEL), F32),
        compiler_params=pltpu.CompilerParams(
            dimension_semantics=("arbitrary",), vmem_limit_bytes=VMEM_LIMIT),
        name="moe_dispatch",
    )(zstart, zflag, dest, x)


def _moe_ffn_kernel(te_ref, nused_ref, x_ref, wg_ref, wu_ref, wd_ref, o_ref):
    i = pl.program_id(0)
    f = pl.program_id(1)

    @pl.when((i >= nused_ref[0]) & (f == 0))
    def _():
        o_ref[...] = jnp.zeros_like(o_ref)

    @pl.when(i < nused_ref[0])
    def _():
        y = _swiglu(x_ref[...].astype(BF16), wg_ref[0], wu_ref[0], wd_ref[0])

        @pl.when(f == 0)
        def _():
            o_ref[...] = y

        @pl.when(f > 0)
        def _():
            o_ref[...] += y


def _moe_ffn(xs, te, nused, wg, wu, wd, tg, tf):
    n_rows = xs.shape[0]
    nf = D_FF // tf
    tile = lambda i, nu: jnp.minimum(i, nu[0] - 1)
    fstep = lambda i, f, nu: jnp.where(i < nu[0], f, nf - 1)
    return pl.pallas_call(
        _moe_ffn_kernel,
        grid_spec=pltpu.PrefetchScalarGridSpec(
            num_scalar_prefetch=2,
            grid=(n_rows // tg, nf),
            in_specs=[
                pl.BlockSpec((tg, D_MODEL), lambda i, f, te, nu: (tile(i, nu), 0)),
                pl.BlockSpec((1, D_MODEL, tf), lambda i, f, te, nu: (te[tile(i, nu)], 0, fstep(i, f, nu))),
                pl.BlockSpec((1, D_MODEL, tf), lambda i, f, te, nu: (te[tile(i, nu)], 0, fstep(i, f, nu))),
                pl.BlockSpec((1, tf, D_MODEL), lambda i, f, te, nu: (te[tile(i, nu)], fstep(i, f, nu), 0)),
            ],
            out_specs=pl.BlockSpec((tg, D_MODEL), lambda i, f, te, nu: (i, 0)),
        ),
        out_shape=jax.ShapeDtypeStruct((n_rows, D_MODEL), F32),
        compiler_params=pltpu.CompilerParams(
            dimension_semantics=("arbitrary", "arbitrary"), vmem_limit_bytes=VMEM_LIMIT),
        name="moe_ffn",
    )(te, nused, xs, wg, wu, wd)


def _combine_kernel(dest_ref, dest_next_ref, route_ref, h_ref, p_ref, npl_ref, wpg_ref, wpl_ref, ys_ref, o_ref,
                    buf_ref, sem, *, tm):
    i = pl.program_id(0)
    cur = i % 2

    def issue_rows(rows_ref, half, base):
        for s in range(TOP_K):
            rows = buf_ref.at[half, s, pl.ds(base, DMA_ROWS)]
            for j in range(DMA_ROWS):
                _row_copy(ys_ref, rows_ref[0, 0, s * tm + base + j], rows, j, sem.at[half]).start()

    def wait_half(half):
        for s in range(TOP_K):
            pltpu.make_async_copy(ys_ref.at[pl.ds(0, tm)], buf_ref.at[half, s], sem.at[half]).wait()

    @pl.when(i == 0)
    def _():
        def issue(g, carry):
            issue_rows(dest_ref, 0, pl.multiple_of(g * DMA_ROWS, DMA_ROWS))
            return carry

        lax.fori_loop(0, tm // DMA_ROWS, issue, 0)

    wait_half(cur)

    g1 = route_ref[:, 2:3]
    g2 = route_ref[:, 3:4]
    h2 = h_ref[...] + (g1 * buf_ref[cur, 0] + g2 * buf_ref[cur, 1])

    for g in range(tm // DMA_ROWS):
        issue_rows(dest_next_ref, 1 - cur, g * DMA_ROWS)

    o_ref[...] = _ple(h2, p_ref, npl_ref, wpg_ref, wpl_ref)

    @pl.when(i == pl.num_programs(0) - 1)
    def _():
        wait_half(1 - cur)


def _combine(ys, dest, route, h, p, layer, npl, wpg, wpl, tm):
    m = h.shape[0]
    row = lambda i: (i, 0)
    fixed = lambda i: (0, 0)
    return pl.pallas_call(
        functools.partial(_combine_kernel, tm=tm),
        grid=(m // tm,),
        in_specs=[
            pl.BlockSpec((1, 1, TOP_K * tm), lambda i: (i, 0, 0), memory_space=pltpu.SMEM),
            pl.BlockSpec((1, 1, TOP_K * tm), lambda i: (jnp.minimum(i + 1, m // tm - 1), 0, 0),
                         memory_space=pltpu.SMEM),
            pl.BlockSpec((tm, LANES), row),
            pl.BlockSpec((tm, D_MODEL), row),
            pl.BlockSpec((None, tm, D_PLE), lambda i: (layer, i, 0)),
            pl.BlockSpec((1, D_MODEL), fixed),
            pl.BlockSpec((D_MODEL, D_MODEL), fixed),
            pl.BlockSpec((D_PLE, D_MODEL), fixed),
            pl.BlockSpec(memory_space=pl.ANY),
        ],
        out_specs=pl.BlockSpec((tm, D_MODEL), row),
        out_shape=jax.ShapeDtypeStruct((m, D_MODEL), F32),
        scratch_shapes=[
            pltpu.VMEM((2, TOP_K, tm, D_MODEL), F32),
            pltpu.SemaphoreType.DMA((2,)),
        ],
        compiler_params=pltpu.CompilerParams(
            dimension_semantics=("arbitrary",), vmem_limit_bytes=VMEM_LIMIT),
        name="moe_combine",
    )(dest, dest, route, h, p, npl, wpg, wpl, ys)


def _routing(route_t, count, m, tm, tg):
    rows = jnp.transpose(route_t, (1, 0, 2)).reshape(SUBLANES, m)
    counts = count[:N_EXPERTS, 0].astype(jnp.int32)
    padded = (counts + tg - 1) // tg * tg
    ends = jnp.cumsum(padded)
    offs = ends - padded

    def dest_rows(expert, rank):
        expert = expert.astype(jnp.int32)
        start = sum(jnp.where(expert == k, offs[k], 0) for k in range(N_EXPERTS))
        return (start + rank.astype(jnp.int32)).reshape(m // tm, 1, tm)

    dest = jnp.concatenate([dest_rows(rows[s], rows[2 * TOP_K + s]) for s in range(TOP_K)], axis=2)
    n_tiles = TOP_K * m // tg + N_EXPERTS
    nused = ends[-1] // tg
    tile_ids = jnp.minimum(jnp.arange(n_tiles, dtype=jnp.int32), nused - 1)
    te = jnp.sum((tile_ids[:, None] >= (ends // tg)[None, :]).astype(jnp.int32), axis=1)
    te = jnp.minimum(te, N_EXPERTS - 1)
    slack = nused + jnp.arange(N_FILL - N_EXPERTS, dtype=jnp.int32)
    zstart = jnp.concatenate([jnp.maximum(ends - tg, 0), jnp.minimum(slack, n_tiles - 1) * tg])
    zflag = jnp.concatenate([counts > 0, slack < n_tiles]).astype(jnp.int32)
    return (dest, te.astype(jnp.int32), nused.reshape(1).astype(jnp.int32),
            zstart.astype(jnp.int32), zflag, n_tiles * tg)


def _moe(hn, h1, p, layer, route, route_t, count, wg, wu, wd, npl, wpg, wpl, tm, tg, tf):
    dest, te, nused, zstart, zflag, n_rows = _routing(route_t, count, hn.shape[0], tm, tg)
    xs = _dispatch(hn, dest, zstart, zflag, n_rows, tm, tg)
    ys = _moe_ffn(xs, te, nused, wg, wu, wd, tg, tf)
    return _combine(ys, dest, route, h1, p, layer, npl, wpg, wpl, tm)


def _alibi_bias():
    slopes = jnp.exp2(-8.0 * jnp.arange(1, N_HEADS + 1, dtype=F32) / N_HEADS)
    qi = jnp.arange(CHUNK, dtype=jnp.int32)[None, :]
    kj = jnp.arange(BAND, dtype=jnp.int32)[:, None]
    dist = jnp.abs(qi + WINDOW - kj).astype(F32)
    bias = slopes[:, None, None] * dist[None]
    bias = bias.reshape(N_KV_HEADS, GROUP, BAND, CHUNK)
    return jnp.transpose(bias, (0, 2, 1, 3)).reshape(N_KV_HEADS, BAND, GROUP * CHUNK)


def _head_sum_matrix():
    head = jnp.arange(QK_WIDTH, dtype=jnp.int32) // HEAD_DIM
    return (head[:, None] == head[None, :]).astype(BF16)


def _layer_params(l, norm_mix, w_in, q_gain, k_gain, attn_sinks, conv_w, conv_b, conv_ln_g, conv_ln_b,
                  out_norm_attn, out_norm_conv, w_out, norm_ffn, router_w, router_b, ple_norm,
                  w_ple_gate, w_ple):
    scale = HEAD_DIM ** -0.5
    prm = {
        "norm_mix": norm_mix[l][None, :],
        "w_in": w_in[l].astype(BF16),
        "qk_gain": jnp.concatenate([jnp.tile(q_gain[l] * scale, N_HEADS), jnp.tile(k_gain[l], N_KV_HEADS)])[None, :],
        "sinks": jnp.broadcast_to(attn_sinks[l].reshape(N_KV_HEADS, 1, GROUP, 1),
                                  (N_KV_HEADS, 1, GROUP, CHUNK)).reshape(N_KV_HEADS, 1, GROUP * CHUNK),
        "conv_w": jnp.pad(conv_w[l], ((0, CONV_PAD - CONV_K), (0, 0))),
        "conv_b": conv_b[l][None, :],
        "cln_g": conv_ln_g[l][None, :],
        "cln_b": conv_ln_b[l][None, :],
        "on_attn": out_norm_attn[l][None, :],
        "on_conv": out_norm_conv[l][None, :],
        "w_out": w_out[l].astype(BF16),
        "norm_ffn": norm_ffn[l][None, :],
        "ple_norm": ple_norm[l][None, :],
        "w_ple_gate": w_ple_gate[l].astype(BF16),
        "w_ple": w_ple[l].astype(BF16),
        "router_w": None,
        "router_b": None,
    }
    if l % 2 == 1:
        i = l // 2
        rw = jnp.pad(router_w[i], ((0, 0), (0, LANES - N_EXPERTS)))
        rw_hi = rw.astype(BF16)
        rw_lo = (rw - rw_hi.astype(F32)).astype(BF16)
        prm["router_w"] = jnp.concatenate([rw_hi, rw_lo], axis=1)
        prm["router_b"] = jnp.pad(router_b[i], (0, LANES - N_EXPERTS))[None, :]
    return prm


def _trunk(x, p, cache_k, cache_v, state_conv, layers, ffn_w, bias, hsum, *, tm, tq, nb=1):
    nseq, seq, _ = x.shape
    m = nseq * seq
    has_past = cache_k is not None
    h = x.reshape(m, D_MODEL)
    p2 = p.reshape(p.shape[0], m, D_PLE)
    win_k, win_v, convs = [], [], []
    for l, prm in enumerate(layers):
        q, kv, u = _in_proj(h, prm["norm_mix"], prm["w_in"], prm["qk_gain"], hsum, tm)
        if has_past:
            win = cache_k.shape[2]
            kv_prev = jnp.concatenate([cache_k[l].reshape(nseq * win, KV_WIDTH),
                                       cache_v[l].reshape(nseq * win, KV_WIDTH)], axis=1)
            u_prev = jnp.pad(state_conv[l], ((0, 0), (CONV_PAD - (CONV_K - 1), 0), (0, 0))).reshape(
                nseq * CONV_PAD, CONV_CH)
        else:
            kv_prev, u_prev = kv, u
        outs = _mixer(q, kv, kv_prev, u, u_prev, h, dict(prm, bias=bias), nseq=nseq, tq=tq, nb=nb, has_past=has_past)
        h1, hn = outs[0], outs[1]
        wg, wu, wd = ffn_w[l]
        if len(outs) > 2:
            route, route_t, count = outs[2:]
            h = _moe(hn, h1, p2, l, route, route_t, count, wg, wu, wd,
                     prm["ple_norm"], prm["w_ple_gate"], prm["w_ple"], tm, tm // 2, D_FF)
        else:
            h = _ffn(hn, h1, p2, l, wg, wu, wd, prm["ple_norm"], prm["w_ple_gate"], prm["w_ple"], tm // 2, D_FF)
        kv3 = kv.reshape(nseq, seq, 2 * KV_WIDTH)
        u3 = u.reshape(nseq, seq, CONV_CH)
        if has_past:
            kv3 = jnp.concatenate([kv_prev.reshape(nseq, win, 2 * KV_WIDTH), kv3], axis=1)[:, -win:]
            u3 = jnp.concatenate([state_conv[l], u3], axis=1)
        else:
            kv3 = kv3[:, seq - WINDOW:]
        win_k.append(kv3[..., :KV_WIDTH].reshape(nseq, -1, N_KV_HEADS, HEAD_DIM))
        win_v.append(kv3[..., KV_WIDTH:].reshape(nseq, -1, N_KV_HEADS, HEAD_DIM))
        convs.append(u3[:, -(CONV_K - 1):])
    return h.reshape(nseq, seq, D_MODEL), jnp.stack(win_k), jnp.stack(win_v), jnp.stack(convs)


def kernel(x_prompt, x_sample, p_prompt, p_sample, cache_k, cache_v, state_conv, norm_mix, w_in, q_gain, k_gain, attn_sinks, conv_w, conv_b, conv_ln_g, conv_ln_b, out_norm_attn, out_norm_conv, w_out, norm_ffn, ffn_gate, ffn_up, ffn_down, router_w, router_b, moe_gate, moe_up, moe_down, ple_norm, w_ple_gate, w_ple):
    depth = w_in.shape[0]
    layers = [
        _layer_params(l, norm_mix, w_in, q_gain, k_gain, attn_sinks, conv_w, conv_b, conv_ln_g, conv_ln_b,
                      out_norm_attn, out_norm_conv, w_out, norm_ffn, router_w, router_b, ple_norm,
                      w_ple_gate, w_ple)
        for l in range(depth)
    ]
    ffn_w = []
    for l in range(depth):
        i = l // 2
        if l % 2 == 0:
            ffn_w.append((ffn_gate[i].astype(BF16), ffn_up[i].astype(BF16), ffn_down[i].astype(BF16)))
        else:
            ffn_w.append((moe_gate[i].astype(BF16), moe_up[i].astype(BF16), moe_down[i].astype(BF16)))
    bias = _alibi_bias()
    hsum = _head_sum_matrix()
    tm_p = min(512, x_prompt.shape[0] * x_prompt.shape[1])
    tq_p = min(512, x_prompt.shape[1])
    y_p, wk_p, wv_p, cv_p = _trunk(x_prompt, p_prompt, None, None, None, layers, ffn_w, bias, hsum,
                                   tm=tm_p, tq=tq_p)
    tm_s = min(512, x_sample.shape[0] * x_sample.shape[1])
    nb_s = max(d for d in range(1, x_sample.shape[0] + 1)
               if x_sample.shape[0] % d == 0 and d * x_sample.shape[1] <= tm_s)
    y_s, wk_s, wv_s, cv_s = _trunk(x_sample, p_sample, cache_k, cache_v, state_conv, layers, ffn_w, bias, hsum,
                                   tm=tm_s, tq=x_sample.shape[1], nb=nb_s)
    return (y_p, y_s, wk_p, wv_p, cv_p, wk_s, wv_s, cv_s)
```

```python
import functools

import jax
import jax.numpy as jnp
from jax import lax
from jax.experimental import pallas as pl
from jax.experimental.pallas import tpu as pltpu

D_MODEL = 1024
CHUNK = 64
WINDOW = 128
N_HEADS = 8
N_KV_HEADS = 2
HEAD_DIM = 64
GROUP = N_HEADS // N_KV_HEADS
HEAD_PACK = 4
ATTN_WIDTH = N_HEADS * HEAD_DIM
KV_WIDTH = N_KV_HEADS * HEAD_DIM
QK_WIDTH = ATTN_WIDTH + KV_WIDTH
CONV_CH = 512
CONV_K = 31
SUBLANES = 8
CONV_PAD = 32
CONV_ROWS = 64
CONV_ACCS = 2
D_IN = ATTN_WIDTH + 2 * KV_WIDTH + 2 * CONV_CH
BAND = WINDOW + CHUNK
D_FF = 2816
N_EXPERTS = 8
TOP_K = 2
N_FILL = 2 * N_EXPERTS
D_PLE = 256
DMA_ROWS = 8
DISPATCH_TILES = 2
EPS = 1e-6
LANES = 128
MXU_WIDTH = 256

F32 = jnp.float32
BF16 = jnp.bfloat16

VMEM_LIMIT = 56 * 1024 * 1024

def _rms(x, g):
    return x * lax.rsqrt(jnp.mean(x * x, axis=-1, keepdims=True) + EPS) * g


def _in_proj_kernel(h_ref, nrm_ref, w_ref, gain_ref, hsum_ref, q_ref, kv_ref, u_ref):
    xn = _rms(h_ref[...], nrm_ref[...])
    z = jnp.dot(xn.astype(BF16), w_ref[...], preferred_element_type=F32)
    qk = z[:, :QK_WIDTH]
    sq = (qk * qk).astype(BF16)
    ss = jnp.concatenate(
        [jnp.dot(sq[:, c:min(c + MXU_WIDTH, QK_WIDTH)],
                 hsum_ref[:min(MXU_WIDTH, QK_WIDTH - c), :min(MXU_WIDTH, QK_WIDTH - c)],
                 preferred_element_type=F32)
         for c in range(0, QK_WIDTH, MXU_WIDTH)], axis=1)
    qkn = qk * lax.rsqrt(ss * (1.0 / HEAD_DIM) + EPS) * gain_ref[...]
    q_ref[...] = qkn[:, :ATTN_WIDTH].astype(q_ref.dtype)
    kv_ref[:, :KV_WIDTH] = qkn[:, ATTN_WIDTH:]
    kv_ref[:, KV_WIDTH:] = z[:, QK_WIDTH:QK_WIDTH + KV_WIDTH]
    a = z[:, QK_WIDTH + KV_WIDTH:QK_WIDTH + KV_WIDTH + CONV_CH]
    gl = z[:, QK_WIDTH + KV_WIDTH + CONV_CH:]
    u_ref[...] = a * jax.nn.sigmoid(gl)


def _in_proj(h, nrm, w_in, qk_gain, hsum, tm):
    m = h.shape[0]
    row = lambda i: (i, 0)
    fixed = lambda i: (0, 0)
    return pl.pallas_call(
        _in_proj_kernel,
        grid=(m // tm,),
        in_specs=[
            pl.BlockSpec((tm, D_MODEL), row),
            pl.BlockSpec((1, D_MODEL), fixed),
            pl.BlockSpec((D_MODEL, D_IN), fixed),
            pl.BlockSpec((1, QK_WIDTH), fixed),
            pl.BlockSpec((MXU_WIDTH, MXU_WIDTH), fixed),
        ],
        out_specs=[
            pl.BlockSpec((tm, ATTN_WIDTH), row),
            pl.BlockSpec((tm, 2 * KV_WIDTH), row),
            pl.BlockSpec((tm, CONV_CH), row),
        ],
        out_shape=[
            jax.ShapeDtypeStruct((m, ATTN_WIDTH), BF16),
            jax.ShapeDtypeStruct((m, 2 * KV_WIDTH), F32),
            jax.ShapeDtypeStruct((m, CONV_CH), F32),
        ],
        compiler_params=pltpu.CompilerParams(
            dimension_semantics=("arbitrary",), vmem_limit_bytes=VMEM_LIMIT),
        name="in_proj",
    )(h, nrm, w_in, qk_gain, hsum)


def _mixer_kernel(q_ref, kv_ref, kvp_ref, u_ref, up_ref, h_ref, bias_ref, sink_ref,
                  cw_ref, cb_ref, lng_ref, lnb_ref, ona_ref, onc_ref, wout_ref, nffn_ref,
                  *rest, tq, nb, has_past, with_router):
    if with_router:
        (rw_ref, rb_ref, utri_ref, h1_ref, hn_ref, route_ref, routet_ref, count_ref,
         kvx_ref, ux_ref, oa_ref, cv_ref, s_ref, p_ref, cnt_ref) = rest
    else:
        h1_ref, hn_ref, kvx_ref, ux_ref, oa_ref, cv_ref, s_ref, p_ref = rest
    t = pl.program_id(1)
    nch = tq // CHUNK
    kv_seg = WINDOW + tq
    u_seg = CONV_PAD + tq

    for i in range(nb):
        kvx_ref[i * kv_seg:i * kv_seg + WINDOW, :] = kvp_ref[i * WINDOW:(i + 1) * WINDOW, :].astype(BF16)
        kvx_ref[i * kv_seg + WINDOW:(i + 1) * kv_seg, :] = kv_ref[i * tq:(i + 1) * tq, :].astype(BF16)

    packs = N_HEADS // HEAD_PACK
    units = [(i * tq + c * CHUNK, i * kv_seg + c * CHUNK, c, hp)
             for i in range(nb) for c in range(nch) for hp in range(packs)]

    def unit_operands(hp):
        j = hp * HEAD_PACK // GROUP
        lanes = slice((hp * HEAD_PACK % GROUP) * CHUNK, (hp * HEAD_PACK % GROUP + HEAD_PACK) * CHUNK)
        return j, lanes

    for u, (qrow, krow, c, hp) in enumerate(units):
        j, lanes = unit_operands(hp)
        qc = q_ref[qrow:qrow + CHUNK, :]
        qs = jnp.concatenate([qc[:, h * HEAD_DIM:(h + 1) * HEAD_DIM]
                              for h in range(hp * HEAD_PACK, (hp + 1) * HEAD_PACK)],
                             axis=0)
        kj = kvx_ref[krow:krow + BAND, j * HEAD_DIM:(j + 1) * HEAD_DIM]
        s = lax.dot_general(kj, qs, (((1,), (1,)), ((), ())), preferred_element_type=F32)
        s = s - bias_ref[j, :, lanes]
        if not has_past and c < WINDOW // CHUNK:
            kpos = lax.broadcasted_iota(jnp.int32, (BAND, 1), 0) + (t * tq + c * CHUNK - WINDOW)
            s = jnp.where(kpos >= 0, s, -jnp.inf)
        s_ref[u] = s

    for u, (qrow, krow, c, hp) in enumerate(units):
        j, lanes = unit_operands(hp)
        s = s_ref[u]
        sink = sink_ref[j, :, lanes]
        mx = jnp.maximum(jnp.max(s, axis=0, keepdims=True), sink)
        e = jnp.exp(s - mx)
        den = jnp.sum(e, axis=0, keepdims=True) + jnp.exp(sink - mx)
        p_ref[u] = (e * (1.0 / den)).astype(BF16)

    for u0 in range(0, len(units), packs):
        outs = []
        for u in range(u0, u0 + packs):
            qrow, krow, c, hp = units[u]
            j, _ = unit_operands(hp)
            vj = kvx_ref[krow:krow + BAND, KV_WIDTH + j * HEAD_DIM:KV_WIDTH + (j + 1) * HEAD_DIM]
            o = lax.dot_general(p_ref[u], vj, (((0,), (0,)), ((), ())),
                                preferred_element_type=F32)
            outs.extend(o[g * CHUNK:(g + 1) * CHUNK, :] for g in range(HEAD_PACK))
        oa_ref[qrow:qrow + CHUNK, :] = jnp.concatenate(outs, axis=1)

    for i in range(nb):
        prefix = up_ref[i * CONV_PAD:(i + 1) * CONV_PAD, :]
        ux_ref[0, i * u_seg:i * u_seg + CONV_PAD, :] = prefix if has_past else jnp.where(t > 0, prefix, 0.0)
        ux_ref[0, i * u_seg + CONV_PAD:(i + 1) * u_seg, :] = u_ref[i * tq:(i + 1) * tq, :]
    shifted_rows = nb * u_seg - SUBLANES
    for r in range(1, SUBLANES):
        ux_ref[r, :shifted_rows, :] = ux_ref[0, r:r + shifted_rows, :]
    lead = CONV_PAD - (CONV_K - 1)

    for cg in range(CONV_CH // LANES):
        lanes = slice(cg * LANES, (cg + 1) * LANES)
        taps = [cw_ref[k:k + 1, lanes] for k in range(CONV_K)]

        def conv_rows(i, carry, lanes=lanes, taps=taps):
            out0 = pl.multiple_of(i * CONV_ROWS, CONV_ROWS)
            r0 = out0 if nb == 1 else pl.multiple_of(i * u_seg, SUBLANES)
            accs = [None] * CONV_ACCS
            for r in range(SUBLANES):
                steps = [(k, (lead + k) // SUBLANES) for k in range(CONV_K) if (lead + k) % SUBLANES == r]
                slab = ux_ref[r, pl.ds(r0, CONV_ROWS + steps[-1][1] * SUBLANES), lanes]
                for k, a in steps:
                    term = slab[a * SUBLANES:a * SUBLANES + CONV_ROWS, :] * taps[k]
                    accs[k % CONV_ACCS] = term if accs[k % CONV_ACCS] is None else accs[k % CONV_ACCS] + term
            cv_ref[pl.ds(out0, CONV_ROWS), lanes] = functools.reduce(lambda x, y: x + y, accs) + cb_ref[:, lanes]
            return carry

        lax.fori_loop(0, nb * tq // CONV_ROWS, conv_rows, 0)

    cv = cv_ref[...]
    mu = jnp.mean(cv, axis=-1, keepdims=True)
    xc = cv - mu
    ln = xc * lax.rsqrt(jnp.mean(xc * xc, axis=-1, keepdims=True) + EPS) * lng_ref[...] + lnb_ref[...]
    oc = ln * jax.nn.sigmoid(ln)
    cn = _rms(oc, onc_ref[...]).astype(BF16)

    an = _rms(oa_ref[...], ona_ref[...]).astype(BF16)
    mixed = jnp.dot(jnp.concatenate([an, cn], axis=1), wout_ref[...], preferred_element_type=F32)
    h1 = h_ref[...] + mixed
    h1_ref[...] = h1
    hn = _rms(h1, nffn_ref[...])
    hn_ref[...] = hn.astype(hn_ref.dtype)

    if with_router:
        hi = hn.astype(BF16)
        lo = (hn - hi.astype(F32)).astype(BF16)
        part = jnp.dot(hi, rw_ref[...], preferred_element_type=F32)
        logits = (part[:, :LANES] + part[:, LANES:]
                  + jnp.dot(lo, rw_ref[:, :LANES], preferred_element_type=F32) + rb_ref[...])
        lt = logits.T[:N_EXPERTS, :]
        row = lax.broadcasted_iota(jnp.int32, lt.shape, 0)
        m1 = jnp.max(lt, axis=0, keepdims=True)
        i1 = jnp.min(jnp.where(lt == m1, row, N_EXPERTS), axis=0, keepdims=True)
        rest_l = jnp.where(row == i1, -jnp.inf, lt)
        m2 = jnp.max(rest_l, axis=0, keepdims=True)
        i2 = jnp.min(jnp.where(rest_l == m2, row, N_EXPERTS), axis=0, keepdims=True)
        e2 = jnp.exp(m2 - m1)
        den = 1.0 + e2

        @pl.when((pl.program_id(0) == 0) & (t == 0))
        def _():
            cnt_ref[...] = jnp.zeros_like(cnt_ref)

        sel = ((row == i1) | (row == i2)).astype(F32)
        ahead = jnp.dot(sel.astype(BF16), utri_ref[...], preferred_element_type=F32) + cnt_ref[:, 0:1]
        r1 = jnp.sum(jnp.where(row == i1, ahead, 0.0), axis=0, keepdims=True)
        r2 = jnp.sum(jnp.where(row == i2, ahead, 0.0), axis=0, keepdims=True)
        cnt_ref[...] += jnp.sum(sel, axis=1, keepdims=True)
        count_ref[...] = cnt_ref[...]

        route_t = jnp.concatenate(
            [i1.astype(F32), i2.astype(F32), 1.0 / den, e2 / den, r1, r2,
             jnp.zeros((SUBLANES - 3 * TOP_K, nb * tq), F32)], axis=0)
        routet_ref[0] = route_t
        route_ref[...] = jnp.concatenate([route_t, jnp.zeros((LANES - SUBLANES, nb * tq), F32)], axis=0).T


def _mixer(q, kv, kv_prev, u, u_prev, h, prm, *, nseq, tq, nb, has_past):
    m = h.shape[0]
    seq = m // nseq
    nt = seq // tq
    assert nb == 1 or (has_past and nt == 1 and tq == CONV_ROWS and nseq % nb == 0)
    tr = nb * tq
    with_router = prm["router_w"] is not None
    cur = lambda b, t: (b * nt + t, 0)
    fixed2 = lambda b, t: (0, 0)
    fixed3 = lambda b, t: (0, 0, 0)
    if has_past:
        kvp_map = lambda b, t: (b, 0)
        up_map = lambda b, t: (b, 0)
    else:
        kvp_map = lambda b, t: (jnp.maximum(b * (seq // WINDOW) + t * (tq // WINDOW) - 1, 0), 0)
        up_map = lambda b, t: (jnp.maximum(b * (seq // CONV_PAD) + t * (tq // CONV_PAD) - 1, 0), 0)
    in_specs = [
        pl.BlockSpec((tr, ATTN_WIDTH), cur),
        pl.BlockSpec((tr, 2 * KV_WIDTH), cur),
        pl.BlockSpec((nb * WINDOW, 2 * KV_WIDTH), kvp_map),
        pl.BlockSpec((tr, CONV_CH), cur),
        pl.BlockSpec((nb * CONV_PAD, CONV_CH), up_map),
        pl.BlockSpec((tr, D_MODEL), cur),
        pl.BlockSpec((N_KV_HEADS, BAND, GROUP * CHUNK), fixed3),
        pl.BlockSpec((N_KV_HEADS, 1, GROUP * CHUNK), fixed3),
        pl.BlockSpec((CONV_PAD, CONV_CH), fixed2),
        pl.BlockSpec((1, CONV_CH), fixed2),
        pl.BlockSpec((1, CONV_CH), fixed2),
        pl.BlockSpec((1, CONV_CH), fixed2),
        pl.BlockSpec((1, ATTN_WIDTH), fixed2),
        pl.BlockSpec((1, CONV_CH), fixed2),
        pl.BlockSpec((D_MODEL, D_MODEL), fixed2),
        pl.BlockSpec((1, D_MODEL), fixed2),
    ]
    args = [q, kv, kv_prev, u, u_prev, h, prm["bias"], prm["sinks"], prm["conv_w"], prm["conv_b"],
            prm["cln_g"], prm["cln_b"], prm["on_attn"], prm["on_conv"], prm["w_out"], prm["norm_ffn"]]
    out_specs = [pl.BlockSpec((tr, D_MODEL), cur), pl.BlockSpec((tr, D_MODEL), cur)]
    out_shape = [jax.ShapeDtypeStruct((m, D_MODEL), F32),
                 jax.ShapeDtypeStruct((m, D_MODEL), F32 if with_router else BF16)]
    n_units = tr // CHUNK * (N_HEADS // HEAD_PACK)
    scratch = [
        pltpu.VMEM((nb * (WINDOW + tq), 2 * KV_WIDTH), BF16),
        pltpu.VMEM((SUBLANES, nb * (CONV_PAD + tq), CONV_CH), F32),
        pltpu.VMEM((tr, ATTN_WIDTH), F32),
        pltpu.VMEM((tr, CONV_CH), F32),
        pltpu.VMEM((n_units, BAND, HEAD_PACK * CHUNK), F32),
        pltpu.VMEM((n_units, BAND, HEAD_PACK * CHUNK), BF16),
    ]
    if with_router:
        utri = (jnp.arange(tr, dtype=jnp.int32)[:, None] < jnp.arange(tr, dtype=jnp.int32)[None, :]).astype(BF16)
        in_specs += [pl.BlockSpec((D_MODEL, 2 * LANES), fixed2), pl.BlockSpec((1, LANES), fixed2),
                     pl.BlockSpec((tr, tr), fixed2)]
        args += [prm["router_w"], prm["router_b"], utri]
        out_specs += [pl.BlockSpec((tr, LANES), cur),
                      pl.BlockSpec((1, SUBLANES, tr), lambda b, t: (b * nt + t, 0, 0)),
                      pl.BlockSpec((SUBLANES, LANES), fixed2)]
        out_shape += [jax.ShapeDtypeStruct((m, LANES), F32),
                      jax.ShapeDtypeStruct((m // tr, SUBLANES, tr), F32),
                      jax.ShapeDtypeStruct((SUBLANES, LANES), F32)]
        scratch.append(pltpu.VMEM((SUBLANES, LANES), F32))
    return pl.pallas_call(
        functools.partial(_mixer_kernel, tq=tq, nb=nb, has_past=has_past, with_router=with_router),
        grid=(nseq // nb, nt),
        in_specs=in_specs,
        out_specs=out_specs,
        out_shape=out_shape,
        scratch_shapes=scratch,
        compiler_params=pltpu.CompilerParams(
            dimension_semantics=("arbitrary", "arbitrary"), vmem_limit_bytes=VMEM_LIMIT),
        name="mixer_past" if has_past else "mixer",
    )(*args)


def _ple(h2, p_ref, npl_ref, wpg_ref, wpl_ref):
    r = _rms(h2, npl_ref[...]).astype(BF16)
    gate = jax.nn.sigmoid(jnp.dot(r, wpg_ref[...], preferred_element_type=F32))
    pe = jnp.dot(p_ref[...].astype(BF16), wpl_ref[...], preferred_element_type=F32)
    return h2 + gate * pe


def _swiglu(x, wg, wu, wd):
    g = jnp.dot(x, wg, preferred_element_type=F32)
    u = jnp.dot(x, wu, preferred_element_type=F32)
    a = (g * jax.nn.sigmoid(g) * u).astype(BF16)
    return jnp.dot(a, wd, preferred_element_type=F32)


def _ffn_kernel(x_ref, h_ref, p_ref, wg_ref, wu_ref, wd_ref, npl_ref, wpg_ref, wpl_ref, o_ref):
    f = pl.program_id(1)

    @pl.when(f == 0)
    def _():
        o_ref[...] = h_ref[...]

    o_ref[...] += _swiglu(x_ref[...], wg_ref[...], wu_ref[...], wd_ref[...])

    @pl.when(f == pl.num_programs(1) - 1)
    def _():
        o_ref[...] = _ple(o_ref[...], p_ref, npl_ref, wpg_ref, wpl_ref)


def _ffn(x, h, p, layer, wg, wu, wd, npl, wpg, wpl, tm, tf):
    m = x.shape[0]
    row = lambda i, f: (i, 0)
    fixed = lambda i, f: (0, 0)
    weight_mode = pl.Buffered(1) if tf == D_FF else None
    return pl.pallas_call(
        _ffn_kernel,
        grid=(m // tm, D_FF // tf),
        in_specs=[
            pl.BlockSpec((tm, D_MODEL), row),
            pl.BlockSpec((tm, D_MODEL), row),
            pl.BlockSpec((None, tm, D_PLE), lambda i, f: (layer, i, 0)),
            pl.BlockSpec((D_MODEL, tf), lambda i, f: (0, f), pipeline_mode=weight_mode),
            pl.BlockSpec((D_MODEL, tf), lambda i, f: (0, f), pipeline_mode=weight_mode),
            pl.BlockSpec((tf, D_MODEL), lambda i, f: (f, 0), pipeline_mode=weight_mode),
            pl.BlockSpec((1, D_MODEL), fixed),
            pl.BlockSpec((D_MODEL, D_MODEL), fixed),
            pl.BlockSpec((D_PLE, D_MODEL), fixed),
        ],
        out_specs=pl.BlockSpec((tm, D_MODEL), row),
        out_shape=jax.ShapeDtypeStruct((m, D_MODEL), F32),
        compiler_params=pltpu.CompilerParams(
            dimension_semantics=("arbitrary", "arbitrary"), vmem_limit_bytes=VMEM_LIMIT),
        name="ffn",
    )(x, h, p, wg, wu, wd, npl, wpg, wpl)


def _row_copy(src_ref, src_row, dst_ref, dst_row, sem):
    return pltpu.make_async_copy(src_ref.at[pl.ds(src_row, 1)], dst_ref.at[pl.ds(dst_row, 1)], sem)


def _dispatch_kernel(zstart_ref, zflag_ref, dest_ref, x_ref, xs_ref, zero_ref, zsem, sem, *, tm, tg, tiles):
    i = pl.program_id(0)

    @pl.when(i == 0)
    def _():
        zero_ref[...] = jnp.zeros_like(zero_ref)
        def fill(e):
            return pltpu.make_async_copy(zero_ref, xs_ref.at[pl.ds(pl.multiple_of(zstart_ref[e], tg), tg)], zsem)

        for e in range(N_FILL):
            @pl.when(zflag_ref[e] > 0)
            def _():
                fill(e).start()
        for e in range(N_FILL):
            @pl.when(zflag_ref[e] > 0)
            def _():
                fill(e).wait()

    for k in range(tiles):
        def issue(g, carry, k=k):
            base = pl.multiple_of(g * DMA_ROWS, DMA_ROWS)
            rows = x_ref.at[pl.ds(k * tm + base, DMA_ROWS)]
            for j in range(DMA_ROWS):
                for s in range(TOP_K):
                    _row_copy(rows, j, xs_ref, dest_ref[k, 0, s * tm + base + j], sem).start()
            return carry

        lax.fori_loop(0, tm // DMA_ROWS, issue, 0)

    for s in range(TOP_K):
        pltpu.make_async_copy(x_ref, xs_ref.at[pl.ds(0, tiles * tm)], sem).wait()


def _dispatch(x, dest, zstart, zflag, n_rows, tm, tg):
    m = x.shape[0]
    tiles = DISPATCH_TILES if (m // tm) % DISPATCH_TILES == 0 else 1
    return pl.pallas_call(
        functools.partial(_dispatch_kernel, tm=tm, tg=tg, tiles=tiles),
        grid_spec=pltpu.PrefetchScalarGridSpec(
            num_scalar_prefetch=2,
            grid=(m // (tiles * tm),),
            in_specs=[
                pl.BlockSpec((tiles, 1, TOP_K * tm), lambda i, zs, zf: (i, 0, 0), memory_space=pltpu.SMEM),
                pl.BlockSpec((tiles * tm, D_MODEL), lambda i, zs, zf: (i, 0)),
            ],
            out_specs=pl.BlockSpec(memory_space=pl.ANY),
            scratch_shapes=[
                pltpu.VMEM((tg, D_MODEL), F32),
                pltpu.SemaphoreType.DMA(()),
                pltpu.SemaphoreType.DMA(()),
            ],
        ),
        out_shape=jax.ShapeDtypeStruct((n_rows, D_MODEL), F32),
        compiler_params=pltpu.CompilerParams(
            dimension_semantics=("arbitrary",), vmem_limit_bytes=VMEM_LIMIT),
        name="moe_dispatch",
    )(zstart, zflag, dest, x)


def _moe_ffn_kernel(te_ref, nused_ref, x_ref, wg_ref, wu_ref, wd_ref, o_ref):
    i = pl.program_id(0)
    f = pl.program_id(1)

    @pl.when((i >= nused_ref[0]) & (f == 0))
    def _():
        o_ref[...] = jnp.zeros_like(o_ref)

    @pl.when(i < nused_ref[0])
    def _():
        y = _swiglu(x_ref[...].astype(BF16), wg_ref[0], wu_ref[0], wd_ref[0])

        @pl.when(f == 0)
        def _():
            o_ref[...] = y

        @pl.when(f > 0)
        def _():
            o_ref[...] += y


def _moe_ffn(xs, te, nused, wg, wu, wd, tg, tf):
    n_rows = xs.shape[0]
    nf = D_FF // tf
    tile = lambda i, nu: jnp.minimum(i, nu[0] - 1)
    fstep = lambda i, f, nu: jnp.where(i < nu[0], f, nf - 1)
    return pl.pallas_call(
        _moe_ffn_kernel,
        grid_spec=pltpu.PrefetchScalarGridSpec(
            num_scalar_prefetch=2,
            grid=(n_rows // tg, nf),
            in_specs=[
                pl.BlockSpec((tg, D_MODEL), lambda i, f, te, nu: (tile(i, nu), 0)),
                pl.BlockSpec((1, D_MODEL, tf), lambda i, f, te, nu: (te[tile(i, nu)], 0, fstep(i, f, nu))),
                pl.BlockSpec((1, D_MODEL, tf), lambda i, f, te, nu: (te[tile(i, nu)], 0, fstep(i, f, nu))),
                pl.BlockSpec((1, tf, D_MODEL), lambda i, f, te, nu: (te[tile(i, nu)], fstep(i, f, nu), 0)),
            ],
            out_specs=pl.BlockSpec((tg, D_MODEL), lambda i, f, te, nu: (i, 0)),
        ),
        out_shape=jax.ShapeDtypeStruct((n_rows, D_MODEL), F32),
        compiler_params=pltpu.CompilerParams(
            dimension_semantics=("arbitrary", "arbitrary"), vmem_limit_bytes=VMEM_LIMIT),
        name="moe_ffn",
    )(te, nused, xs, wg, wu, wd)


def _combine_kernel(dest_ref, dest_next_ref, route_ref, h_ref, p_ref, npl_ref, wpg_ref, wpl_ref, ys_ref, o_ref,
                    buf_ref, sem, *, tm):
    i = pl.program_id(0)
    cur = i % 2

    def issue_rows(rows_ref, half, base):
        for s in range(TOP_K):
            rows = buf_ref.at[half, s, pl.ds(base, DMA_ROWS)]
            for j in range(DMA_ROWS):
                _row_copy(ys_ref, rows_ref[0, 0, s * tm + base + j], rows, j, sem.at[half]).start()

    def wait_half(half):
        for s in range(TOP_K):
            pltpu.make_async_copy(ys_ref.at[pl.ds(0, tm)], buf_ref.at[half, s], sem.at[half]).wait()

    @pl.when(i == 0)
    def _():
        def issue(g, carry):
            issue_rows(dest_ref, 0, pl.multiple_of(g * DMA_ROWS, DMA_ROWS))
            return carry

        lax.fori_loop(0, tm // DMA_ROWS, issue, 0)

    wait_half(cur)

    g1 = route_ref[:, 2:3]
    g2 = route_ref[:, 3:4]
    h2 = h_ref[...] + (g1 * buf_ref[cur, 0] + g2 * buf_ref[cur, 1])

    for g in range(tm // DMA_ROWS):
        issue_rows(dest_next_ref, 1 - cur, g * DMA_ROWS)

    o_ref[...] = _ple(h2, p_ref, npl_ref, wpg_ref, wpl_ref)

    @pl.when(i == pl.num_programs(0) - 1)
    def _():
        wait_half(1 - cur)


def _combine(ys, dest, route, h, p, layer, npl, wpg, wpl, tm):
    m = h.shape[0]
    row = lambda i: (i, 0)
    fixed = lambda i: (0, 0)
    return pl.pallas_call(
        functools.partial(_combine_kernel, tm=tm),
        grid=(m // tm,),
        in_specs=[
            pl.BlockSpec((1, 1, TOP_K * tm), lambda i: (i, 0, 0), memory_space=pltpu.SMEM),
            pl.BlockSpec((1, 1, TOP_K * tm), lambda i: (jnp.minimum(i + 1, m // tm - 1), 0, 0),
                         memory_space=pltpu.SMEM),
            pl.BlockSpec((tm, LANES), row),
            pl.BlockSpec((tm, D_MODEL), row),
            pl.BlockSpec((None, tm, D_PLE), lambda i: (layer, i, 0)),
            pl.BlockSpec((1, D_MODEL), fixed),
            pl.BlockSpec((D_MODEL, D_MODEL), fixed),
            pl.BlockSpec((D_PLE, D_MODEL), fixed),
            pl.BlockSpec(memory_space=pl.ANY),
        ],
        out_specs=pl.BlockSpec((tm, D_MODEL), row),
        out_shape=jax.ShapeDtypeStruct((m, D_MODEL), F32),
        scratch_shapes=[
            pltpu.VMEM((2, TOP_K, tm, D_MODEL), F32),
            pltpu.SemaphoreType.DMA((2,)),
        ],
        compiler_params=pltpu.CompilerParams(
            dimension_semantics=("arbitrary",), vmem_limit_bytes=VMEM_LIMIT),
        name="moe_combine",
    )(dest, dest, route, h, p, npl, wpg, wpl, ys)


def _routing(route_t, count, m, tm, tg):
    rows = jnp.transpose(route_t, (1, 0, 2)).reshape(SUBLANES, m)
    counts = count[:N_EXPERTS, 0].astype(jnp.int32)
    padded = (counts + tg - 1) // tg * tg
    ends = jnp.cumsum(padded)
    offs = ends - padded

    def dest_rows(expert, rank):
        expert = expert.astype(jnp.int32)
        start = sum(jnp.where(expert == k, offs[k], 0) for k in range(N_EXPERTS))
        return (start + rank.astype(jnp.int32)).reshape(m // tm, 1, tm)

    dest = jnp.concatenate([dest_rows(rows[s], rows[2 * TOP_K + s]) for s in range(TOP_K)], axis=2)
    n_tiles = TOP_K * m // tg + N_EXPERTS
    nused = ends[-1] // tg
    tile_ids = jnp.minimum(jnp.arange(n_tiles, dtype=jnp.int32), nused - 1)
    te = jnp.sum((tile_ids[:, None] >= (ends // tg)[None, :]).astype(jnp.int32), axis=1)
    te = jnp.minimum(te, N_EXPERTS - 1)
    slack = nused + jnp.arange(N_FILL - N_EXPERTS, dtype=jnp.int32)
    zstart = jnp.concatenate([jnp.maximum(ends - tg, 0), jnp.minimum(slack, n_tiles - 1) * tg])
    zflag = jnp.concatenate([counts > 0, slack < n_tiles]).astype(jnp.int32)
    return (dest, te.astype(jnp.int32), nused.reshape(1).astype(jnp.int32),
            zstart.astype(jnp.int32), zflag, n_tiles * tg)


def _moe(hn, h1, p, layer, route, route_t, count, wg, wu, wd, npl, wpg, wpl, tm, tg, tf):
    dest, te, nused, zstart, zflag, n_rows = _routing(route_t, count, hn.shape[0], tm, tg)
    xs = _dispatch(hn, dest, zstart, zflag, n_rows, tm, tg)
    ys = _moe_ffn(xs, te, nused, wg, wu, wd, tg, tf)
    return _combine(ys, dest, route, h1, p, layer, npl, wpg, wpl, tm)


def _alibi_bias():
    slopes = jnp.exp2(-8.0 * jnp.arange(1, N_HEADS + 1, dtype=F32) / N_HEADS)
    qi = jnp.arange(CHUNK, dtype=jnp.int32)[None, :]
    kj = jnp.arange(BAND, dtype=jnp.int32)[:, None]
    dist = jnp.abs(qi + WINDOW - kj).astype(F32)
    bias = slopes[:, None, None] * dist[None]
    bias = bias.reshape(N_KV_HEADS, GROUP, BAND, CHUNK)
    return jnp.transpose(bias, (0, 2, 1, 3)).reshape(N_KV_HEADS, BAND, GROUP * CHUNK)


def _head_sum_matrix():
    head = jnp.arange(MXU_WIDTH, dtype=jnp.int32) // HEAD_DIM
    return (head[:, None] == head[None, :]).astype(BF16)


def _layer_params(l, norm_mix, w_in, q_gain, k_gain, attn_sinks, conv_w, conv_b, conv_ln_g, conv_ln_b,
                  out_norm_attn, out_norm_conv, w_out, norm_ffn, router_w, router_b, ple_norm,
                  w_ple_gate, w_ple):
    scale = HEAD_DIM ** -0.5
    prm = {
        "norm_mix": norm_mix[l][None, :],
        "w_in": w_in[l].astype(BF16),
        "qk_gain": jnp.concatenate([jnp.tile(q_gain[l] * scale, N_HEADS), jnp.tile(k_gain[l], N_KV_HEADS)])[None, :],
        "sinks": jnp.broadcast_to(attn_sinks[l].reshape(N_KV_HEADS, 1, GROUP, 1),
                                  (N_KV_HEADS, 1, GROUP, CHUNK)).reshape(N_KV_HEADS, 1, GROUP * CHUNK),
        "conv_w": jnp.pad(conv_w[l], ((0, CONV_PAD - CONV_K), (0, 0))),
        "conv_b": conv_b[l][None, :],
        "cln_g": conv_ln_g[l][None, :],
        "cln_b": conv_ln_b[l][None, :],
        "on_attn": out_norm_attn[l][None, :],
        "on_conv": out_norm_conv[l][None, :],
        "w_out": w_out[l].astype(BF16),
        "norm_ffn": norm_ffn[l][None, :],
        "ple_norm": ple_norm[l][None, :],
        "w_ple_gate": w_ple_gate[l].astype(BF16),
        "w_ple": w_ple[l].astype(BF16),
        "router_w": None,
        "router_b": None,
    }
    if l % 2 == 1:
        i = l // 2
        rw = jnp.pad(router_w[i], ((0, 0), (0, LANES - N_EXPERTS)))
        rw_hi = rw.astype(BF16)
        rw_lo = (rw - rw_hi.astype(F32)).astype(BF16)
        prm["router_w"] = jnp.concatenate([rw_hi, rw_lo], axis=1)
        prm["router_b"] = jnp.pad(router_b[i], (0, LANES - N_EXPERTS))[None, :]
    return prm


def _trunk(x, p, cache_k, cache_v, state_conv, layers, ffn_w, bias, hsum, *, tm, tq, nb=1):
    nseq, seq, _ = x.shape
    m = nseq * seq
    has_past = cache_k is not None
    h = x.reshape(m, D_MODEL)
    p2 = p.reshape(p.shape[0], m, D_PLE)
    win_k, win_v, convs = [], [], []
    for l, prm in enumerate(layers):
        q, kv, u = _in_proj(h, prm["norm_mix"], prm["w_in"], prm["qk_gain"], hsum, tm)
        if has_past:
            win = cache_k.shape[2]
            kv_prev = jnp.concatenate([cache_k[l].reshape(nseq * win, KV_WIDTH),
                                       cache_v[l].reshape(nseq * win, KV_WIDTH)], axis=1)
            u_prev = jnp.pad(state_conv[l], ((0, 0), (CONV_PAD - (CONV_K - 1), 0), (0, 0))).reshape(
                nseq * CONV_PAD, CONV_CH)
        else:
            kv_prev, u_prev = kv, u
        outs = _mixer(q, kv, kv_prev, u, u_prev, h, dict(prm, bias=bias), nseq=nseq, tq=tq, nb=nb, has_past=has_past)
        h1, hn = outs[0], outs[1]
        wg, wu, wd = ffn_w[l]
        if len(outs) > 2:
            route, route_t, count = outs[2:]
            h = _moe(hn, h1, p2, l, route, route_t, count, wg, wu, wd,
                     prm["ple_norm"], prm["w_ple_gate"], prm["w_ple"], tm, tm // 2, D_FF)
        else:
            h = _ffn(hn, h1, p2, l, wg, wu, wd, prm["ple_norm"], prm["w_ple_gate"], prm["w_ple"], tm // 2, D_FF)
        kv3 = kv.reshape(nseq, seq, 2 * KV_WIDTH)
        u3 = u.reshape(nseq, seq, CONV_CH)
        if has_past:
            kv3 = jnp.concatenate([kv_prev.reshape(nseq, win, 2 * KV_WIDTH), kv3], axis=1)[:, -win:]
            u3 = jnp.concatenate([state_conv[l], u3], axis=1)
        else:
            kv3 = kv3[:, seq - WINDOW:]
        win_k.append(kv3[..., :KV_WIDTH].reshape(nseq, -1, N_KV_HEADS, HEAD_DIM))
        win_v.append(kv3[..., KV_WIDTH:].reshape(nseq, -1, N_KV_HEADS, HEAD_DIM))
        convs.append(u3[:, -(CONV_K - 1):])
    return h.reshape(nseq, seq, D_MODEL), jnp.stack(win_k), jnp.stack(win_v), jnp.stack(convs)


def kernel(x_prompt, x_sample, p_prompt, p_sample, cache_k, cache_v, state_conv, norm_mix, w_in, q_gain, k_gain, attn_sinks, conv_w, conv_b, conv_ln_g, conv_ln_b, out_norm_attn, out_norm_conv, w_out, norm_ffn, ffn_gate, ffn_up, ffn_down, router_w, router_b, moe_gate, moe_up, moe_down, ple_norm, w_ple_gate, w_ple):
    depth = w_in.shape[0]
    layers = [
        _layer_params(l, norm_mix, w_in, q_gain, k_gain, attn_sinks, conv_w, conv_b, conv_ln_g, conv_ln_b,
                      out_norm_attn, out_norm_conv, w_out, norm_ffn, router_w, router_b, ple_norm,
                      w_ple_gate, w_ple)
        for l in range(depth)
    ]
    ffn_w = []
    for l in range(depth):
        i = l // 2
        if l % 2 == 0:
            ffn_w.append((ffn_gate[i].astype(BF16), ffn_up[i].astype(BF16), ffn_down[i].astype(BF16)))
        else:
            ffn_w.append((moe_gate[i].astype(BF16), moe_up[i].astype(BF16), moe_down[i].astype(BF16)))
    bias = _alibi_bias()
    hsum = _head_sum_matrix()
    tm_p = min(512, x_prompt.shape[0] * x_prompt.shape[1])
    tq_p = min(512, x_prompt.shape[1])
    y_p, wk_p, wv_p, cv_p = _trunk(x_prompt, p_prompt, None, None, None, layers, ffn_w, bias, hsum,
                                   tm=tm_p, tq=tq_p)
    tm_s = min(512, x_sample.shape[0] * x_sample.shape[1])
    nb_s = max(d for d in range(1, x_sample.shape[0] + 1)
               if x_sample.shape[0] % d == 0 and d * x_sample.shape[1] <= tm_s)
    y_s, wk_s, wv_s, cv_s = _trunk(x_sample, p_sample, cache_k, cache_v, state_conv, layers, ffn_w, bias, hsum,
                                   tm=tm_s, tq=x_sample.shape[1], nb=nb_s)
    return (y_p, y_s, wk_p, wv_p, cv_p, wk_s, wv_s, cv_s)
```

```python
import functools

import jax
import jax.numpy as jnp
from jax import lax
from jax.experimental import pallas as pl
from jax.experimental.pallas import tpu as pltpu

D_MODEL = 1024
CHUNK = 64
WINDOW = 128
N_HEADS = 8
N_KV_HEADS = 2
HEAD_DIM = 64
GROUP = N_HEADS // N_KV_HEADS
HEAD_PACK = 4
ATTN_WIDTH = N_HEADS * HEAD_DIM
KV_WIDTH = N_KV_HEADS * HEAD_DIM
QK_WIDTH = ATTN_WIDTH + KV_WIDTH
CONV_CH = 512
CONV_K = 31
SUBLANES = 8
CONV_PAD = 32
CONV_ROWS = 64
CONV_ACCS = 4
D_IN = ATTN_WIDTH + 2 * KV_WIDTH + 2 * CONV_CH
BAND = WINDOW + CHUNK
D_FF = 2816
N_EXPERTS = 8
TOP_K = 2
N_FILL = 2 * N_EXPERTS
D_PLE = 256
DMA_ROWS = 8
DISPATCH_TILES = 2
EPS = 1e-6
LANES = 128
MXU_WIDTH = 256
TOKEN_TILE = 512

F32 = jnp.float32
BF16 = jnp.bfloat16

VMEM_LIMIT = 56 * 1024 * 1024

def _rms(x, g):
    return x * lax.rsqrt(jnp.mean(x * x, axis=-1, keepdims=True) + EPS) * g


def _in_proj_kernel(h_ref, nrm_ref, w_ref, gain_ref, hsum_ref, q_ref, kv_ref, u_ref):
    xn = _rms(h_ref[...], nrm_ref[...])
    z = jnp.dot(xn.astype(BF16), w_ref[...], preferred_element_type=F32)
    qk = z[:, :QK_WIDTH]
    sq = (qk * qk).astype(BF16)
    ss = jnp.concatenate(
        [jnp.dot(sq[:, c:min(c + MXU_WIDTH, QK_WIDTH)],
                 hsum_ref[:min(MXU_WIDTH, QK_WIDTH - c), :min(MXU_WIDTH, QK_WIDTH - c)],
                 preferred_element_type=F32)
         for c in range(0, QK_WIDTH, MXU_WIDTH)], axis=1)
    qkn = qk * lax.rsqrt(ss * (1.0 / HEAD_DIM) + EPS) * gain_ref[...]
    q_ref[...] = qkn[:, :ATTN_WIDTH].astype(q_ref.dtype)
    kv_ref[:, :KV_WIDTH] = qkn[:, ATTN_WIDTH:]
    kv_ref[:, KV_WIDTH:] = z[:, QK_WIDTH:QK_WIDTH + KV_WIDTH]
    a = z[:, QK_WIDTH + KV_WIDTH:QK_WIDTH + KV_WIDTH + CONV_CH]
    gl = z[:, QK_WIDTH + KV_WIDTH + CONV_CH:]
    u_ref[...] = a * jax.nn.sigmoid(gl)


def _in_proj(h, nrm, w_in, qk_gain, hsum, tm):
    m = h.shape[0]
    row = lambda i: (i, 0)
    fixed = lambda i: (0, 0)
    return pl.pallas_call(
        _in_proj_kernel,
        grid=(m // tm,),
        in_specs=[
            pl.BlockSpec((tm, D_MODEL), row),
            pl.BlockSpec((1, D_MODEL), fixed),
            pl.BlockSpec((D_MODEL, D_IN), fixed),
            pl.BlockSpec((1, QK_WIDTH), fixed),
            pl.BlockSpec((MXU_WIDTH, MXU_WIDTH), fixed),
        ],
        out_specs=[
            pl.BlockSpec((tm, ATTN_WIDTH), row),
            pl.BlockSpec((tm, 2 * KV_WIDTH), row),
            pl.BlockSpec((tm, CONV_CH), row),
        ],
        out_shape=[
            jax.ShapeDtypeStruct((m, ATTN_WIDTH), BF16),
            jax.ShapeDtypeStruct((m, 2 * KV_WIDTH), F32),
            jax.ShapeDtypeStruct((m, CONV_CH), F32),
        ],
        compiler_params=pltpu.CompilerParams(
            dimension_semantics=("arbitrary",), vmem_limit_bytes=VMEM_LIMIT),
        name="in_proj",
    )(h, nrm, w_in, qk_gain, hsum)


def _mixer_kernel(q_ref, kv_ref, kvp_ref, u_ref, up_ref, h_ref, bias_ref, sink_ref,
                  cw_ref, cb_ref, lng_ref, lnb_ref, ona_ref, onc_ref, wout_ref, nffn_ref,
                  *rest, tq, nb, has_past, with_router):
    if with_router:
        (rw_ref, rb_ref, utri_ref, h1_ref, hn_ref, route_ref, routet_ref, count_ref,
         kvx_ref, ux_ref, oa_ref, cv_ref, s_ref, p_ref, cnt_ref) = rest
    else:
        h1_ref, hn_ref, kvx_ref, ux_ref, oa_ref, cv_ref, s_ref, p_ref = rest
    t = pl.program_id(1)
    nch = tq // CHUNK
    kv_seg = WINDOW + tq
    u_seg = CONV_PAD + tq

    for i in range(nb):
        kvx_ref[i * kv_seg:i * kv_seg + WINDOW, :] = kvp_ref[i * WINDOW:(i + 1) * WINDOW, :].astype(BF16)
        kvx_ref[i * kv_seg + WINDOW:(i + 1) * kv_seg, :] = kv_ref[i * tq:(i + 1) * tq, :].astype(BF16)

    packs = N_HEADS // HEAD_PACK
    units = [(i * tq + c * CHUNK, i * kv_seg + c * CHUNK, c, hp)
             for i in range(nb) for c in range(nch) for hp in range(packs)]

    def unit_operands(hp):
        j = hp * HEAD_PACK // GROUP
        lanes = slice((hp * HEAD_PACK % GROUP) * CHUNK, (hp * HEAD_PACK % GROUP + HEAD_PACK) * CHUNK)
        return j, lanes

    for u, (qrow, krow, c, hp) in enumerate(units):
        j, lanes = unit_operands(hp)
        qc = q_ref[qrow:qrow + CHUNK, :]
        qs = jnp.concatenate([qc[:, h * HEAD_DIM:(h + 1) * HEAD_DIM]
                              for h in range(hp * HEAD_PACK, (hp + 1) * HEAD_PACK)],
                             axis=0)
        kj = kvx_ref[krow:krow + BAND, j * HEAD_DIM:(j + 1) * HEAD_DIM]
        s = lax.dot_general(kj, qs, (((1,), (1,)), ((), ())), preferred_element_type=F32)
        s = s - bias_ref[j, :, lanes]
        if not has_past and c < WINDOW // CHUNK:
            kpos = lax.broadcasted_iota(jnp.int32, (BAND, 1), 0) + (t * tq + c * CHUNK - WINDOW)
            s = jnp.where(kpos >= 0, s, -jnp.inf)
        s_ref[u] = s

    for u, (qrow, krow, c, hp) in enumerate(units):
        j, lanes = unit_operands(hp)
        s = s_ref[u]
        sink = sink_ref[j, :, lanes]
        mx = jnp.maximum(jnp.max(s, axis=0, keepdims=True), sink)
        e = jnp.exp(s - mx)
        den = jnp.sum(e, axis=0, keepdims=True) + jnp.exp(sink - mx)
        p_ref[u] = (e * (1.0 / den)).astype(BF16)

    for u0 in range(0, len(units), packs):
        outs = []
        for u in range(u0, u0 + packs):
            qrow, krow, c, hp = units[u]
            j, _ = unit_operands(hp)
            vj = kvx_ref[krow:krow + BAND, KV_WIDTH + j * HEAD_DIM:KV_WIDTH + (j + 1) * HEAD_DIM]
            o = lax.dot_general(p_ref[u], vj, (((0,), (0,)), ((), ())),
                                preferred_element_type=F32)
            outs.extend(o[g * CHUNK:(g + 1) * CHUNK, :] for g in range(HEAD_PACK))
        oa_ref[qrow:qrow + CHUNK, :] = jnp.concatenate(outs, axis=1)

    for i in range(nb):
        prefix = up_ref[i * CONV_PAD:(i + 1) * CONV_PAD, :]
        ux_ref[0, i * u_seg:i * u_seg + CONV_PAD, :] = prefix if has_past else jnp.where(t > 0, prefix, 0.0)
        ux_ref[0, i * u_seg + CONV_PAD:(i + 1) * u_seg, :] = u_ref[i * tq:(i + 1) * tq, :]
    shifted_rows = nb * u_seg - SUBLANES
    for r in range(1, SUBLANES):
        ux_ref[r, :shifted_rows, :] = ux_ref[0, r:r + shifted_rows, :]
    lead = CONV_PAD - (CONV_K - 1)

    for cg in range(CONV_CH // LANES):
        lanes = slice(cg * LANES, (cg + 1) * LANES)
        taps = [cw_ref[k:k + 1, lanes] for k in range(CONV_K)]

        def conv_rows(i, carry, lanes=lanes, taps=taps):
            out0 = pl.multiple_of(i * CONV_ROWS, CONV_ROWS)
            r0 = out0 if nb == 1 else pl.multiple_of(i * u_seg, SUBLANES)
            accs = [None] * CONV_ACCS
            for r in range(SUBLANES):
                steps = [(k, (lead + k) // SUBLANES) for k in range(CONV_K) if (lead + k) % SUBLANES == r]
                slab = ux_ref[r, pl.ds(r0, CONV_ROWS + steps[-1][1] * SUBLANES), lanes]
                for k, a in steps:
                    term = slab[a * SUBLANES:a * SUBLANES + CONV_ROWS, :] * taps[k]
                    accs[k % CONV_ACCS] = term if accs[k % CONV_ACCS] is None else accs[k % CONV_ACCS] + term
            cv_ref[pl.ds(out0, CONV_ROWS), lanes] = functools.reduce(lambda x, y: x + y, accs) + cb_ref[:, lanes]
            return carry

        lax.fori_loop(0, nb * tq // CONV_ROWS, conv_rows, 0)

    cv = cv_ref[...]
    mu = jnp.mean(cv, axis=-1, keepdims=True)
    xc = cv - mu
    ln = xc * lax.rsqrt(jnp.mean(xc * xc, axis=-1, keepdims=True) + EPS) * lng_ref[...] + lnb_ref[...]
    oc = ln * jax.nn.sigmoid(ln)
    cn = _rms(oc, onc_ref[...]).astype(BF16)

    an = _rms(oa_ref[...], ona_ref[...]).astype(BF16)
    mixed = jnp.dot(jnp.concatenate([an, cn], axis=1), wout_ref[...], preferred_element_type=F32)
    h1 = h_ref[...] + mixed
    h1_ref[...] = h1
    hn = _rms(h1, nffn_ref[...])
    hn_ref[...] = hn.astype(hn_ref.dtype)

    if with_router:
        hi = hn.astype(BF16)
        lo = (hn - hi.astype(F32)).astype(BF16)
        part = jnp.dot(hi, rw_ref[...], preferred_element_type=F32)
        logits = (part[:, :LANES] + part[:, LANES:]
                  + jnp.dot(lo, rw_ref[:, :LANES], preferred_element_type=F32) + rb_ref[...])
        lt = logits.T[:N_EXPERTS, :]
        row = lax.broadcasted_iota(jnp.int32, lt.shape, 0)
        m1 = jnp.max(lt, axis=0, keepdims=True)
        i1 = jnp.min(jnp.where(lt == m1, row, N_EXPERTS), axis=0, keepdims=True)
        rest_l = jnp.where(row == i1, -jnp.inf, lt)
        m2 = jnp.max(rest_l, axis=0, keepdims=True)
        i2 = jnp.min(jnp.where(rest_l == m2, row, N_EXPERTS), axis=0, keepdims=True)
        e2 = jnp.exp(m2 - m1)
        den = 1.0 + e2

        @pl.when((pl.program_id(0) == 0) & (t == 0))
        def _():
            cnt_ref[...] = jnp.zeros_like(cnt_ref)

        sel = ((row == i1) | (row == i2)).astype(F32)
        ahead = jnp.dot(sel.astype(BF16), utri_ref[...], preferred_element_type=F32) + cnt_ref[:, 0:1]
        r1 = jnp.sum(jnp.where(row == i1, ahead, 0.0), axis=0, keepdims=True)
        r2 = jnp.sum(jnp.where(row == i2, ahead, 0.0), axis=0, keepdims=True)
        cnt_ref[...] += jnp.sum(sel, axis=1, keepdims=True)
        count_ref[...] = cnt_ref[...]

        route_t = jnp.concatenate(
            [i1.astype(F32), i2.astype(F32), 1.0 / den, e2 / den, r1, r2,
             jnp.zeros((SUBLANES - 3 * TOP_K, nb * tq), F32)], axis=0)
        routet_ref[0] = route_t
        route_ref[...] = jnp.concatenate([route_t, jnp.zeros((LANES - SUBLANES, nb * tq), F32)], axis=0).T


def _mixer(q, kv, kv_prev, u, u_prev, h, prm, *, nseq, tq, nb, has_past):
    m = h.shape[0]
    seq = m // nseq
    nt = seq // tq
    assert nb == 1 or (has_past and nt == 1 and tq == CONV_ROWS and nseq % nb == 0)
    tr = nb * tq
    with_router = prm["router_w"] is not None
    cur = lambda b, t: (b * nt + t, 0)
    fixed2 = lambda b, t: (0, 0)
    fixed3 = lambda b, t: (0, 0, 0)
    if has_past:
        kvp_map = lambda b, t: (b, 0)
        up_map = lambda b, t: (b, 0)
    else:
        kvp_map = lambda b, t: (jnp.maximum(b * (seq // WINDOW) + t * (tq // WINDOW) - 1, 0), 0)
        up_map = lambda b, t: (jnp.maximum(b * (seq // CONV_PAD) + t * (tq // CONV_PAD) - 1, 0), 0)
    in_specs = [
        pl.BlockSpec((tr, ATTN_WIDTH), cur),
        pl.BlockSpec((tr, 2 * KV_WIDTH), cur),
        pl.BlockSpec((nb * WINDOW, 2 * KV_WIDTH), kvp_map),
        pl.BlockSpec((tr, CONV_CH), cur),
        pl.BlockSpec((nb * CONV_PAD, CONV_CH), up_map),
        pl.BlockSpec((tr, D_MODEL), cur),
        pl.BlockSpec((N_KV_HEADS, BAND, GROUP * CHUNK), fixed3),
        pl.BlockSpec((N_KV_HEADS, 1, GROUP * CHUNK), fixed3),
        pl.BlockSpec((CONV_PAD, CONV_CH), fixed2),
        pl.BlockSpec((1, CONV_CH), fixed2),
        pl.BlockSpec((1, CONV_CH), fixed2),
        pl.BlockSpec((1, CONV_CH), fixed2),
        pl.BlockSpec((1, ATTN_WIDTH), fixed2),
        pl.BlockSpec((1, CONV_CH), fixed2),
        pl.BlockSpec((D_MODEL, D_MODEL), fixed2),
        pl.BlockSpec((1, D_MODEL), fixed2),
    ]
    args = [q, kv, kv_prev, u, u_prev, h, prm["bias"], prm["sinks"], prm["conv_w"], prm["conv_b"],
            prm["cln_g"], prm["cln_b"], prm["on_attn"], prm["on_conv"], prm["w_out"], prm["norm_ffn"]]
    out_specs = [pl.BlockSpec((tr, D_MODEL), cur), pl.BlockSpec((tr, D_MODEL), cur)]
    out_shape = [jax.ShapeDtypeStruct((m, D_MODEL), F32),
                 jax.ShapeDtypeStruct((m, D_MODEL), F32 if with_router else BF16)]
    n_units = tr // CHUNK * (N_HEADS // HEAD_PACK)
    scratch = [
        pltpu.VMEM((nb * (WINDOW + tq), 2 * KV_WIDTH), BF16),
        pltpu.VMEM((SUBLANES, nb * (CONV_PAD + tq), CONV_CH), F32),
        pltpu.VMEM((tr, ATTN_WIDTH), F32),
        pltpu.VMEM((tr, CONV_CH), F32),
        pltpu.VMEM((n_units, BAND, HEAD_PACK * CHUNK), F32),
        pltpu.VMEM((n_units, BAND, HEAD_PACK * CHUNK), BF16),
    ]
    if with_router:
        utri = (jnp.arange(tr, dtype=jnp.int32)[:, None] < jnp.arange(tr, dtype=jnp.int32)[None, :]).astype(BF16)
        in_specs += [pl.BlockSpec((D_MODEL, 2 * LANES), fixed2), pl.BlockSpec((1, LANES), fixed2),
                     pl.BlockSpec((tr, tr), fixed2)]
        args += [prm["router_w"], prm["router_b"], utri]
        out_specs += [pl.BlockSpec((tr, LANES), cur),
                      pl.BlockSpec((1, SUBLANES, tr), lambda b, t: (b * nt + t, 0, 0)),
                      pl.BlockSpec((SUBLANES, LANES), fixed2)]
        out_shape += [jax.ShapeDtypeStruct((m, LANES), F32),
                      jax.ShapeDtypeStruct((m // tr, SUBLANES, tr), F32),
                      jax.ShapeDtypeStruct((SUBLANES, LANES), F32)]
        scratch.append(pltpu.VMEM((SUBLANES, LANES), F32))
    return pl.pallas_call(
        functools.partial(_mixer_kernel, tq=tq, nb=nb, has_past=has_past, with_router=with_router),
        grid=(nseq // nb, nt),
        in_specs=in_specs,
        out_specs=out_specs,
        out_shape=out_shape,
        scratch_shapes=scratch,
        compiler_params=pltpu.CompilerParams(
            dimension_semantics=("arbitrary", "arbitrary"), vmem_limit_bytes=VMEM_LIMIT),
        name="mixer_past" if has_past else "mixer",
    )(*args)


def _ple(h2, p_ref, npl_ref, wpg_ref, wpl_ref):
    r = _rms(h2, npl_ref[...]).astype(BF16)
    gate = jax.nn.sigmoid(jnp.dot(r, wpg_ref[...], preferred_element_type=F32))
    pe = jnp.dot(p_ref[...].astype(BF16), wpl_ref[...], preferred_element_type=F32)
    return h2 + gate * pe


def _swiglu(x, wg, wu, wd):
    g = jnp.dot(x, wg, preferred_element_type=F32)
    u = jnp.dot(x, wu, preferred_element_type=F32)
    a = (g * jax.nn.sigmoid(g) * u).astype(BF16)
    return jnp.dot(a, wd, preferred_element_type=F32)


def _ffn_kernel(x_ref, h_ref, p_ref, wg_ref, wu_ref, wd_ref, npl_ref, wpg_ref, wpl_ref, o_ref):
    f = pl.program_id(1)

    @pl.when(f == 0)
    def _():
        o_ref[...] = h_ref[...]

    o_ref[...] += _swiglu(x_ref[...], wg_ref[...], wu_ref[...], wd_ref[...])

    @pl.when(f == pl.num_programs(1) - 1)
    def _():
        o_ref[...] = _ple(o_ref[...], p_ref, npl_ref, wpg_ref, wpl_ref)


def _ffn(x, h, p, layer, wg, wu, wd, npl, wpg, wpl, tm, tf):
    m = x.shape[0]
    row = lambda i, f: (i, 0)
    fixed = lambda i, f: (0, 0)
    weight_mode = pl.Buffered(1) if tf == D_FF else None
    return pl.pallas_call(
        _ffn_kernel,
        grid=(m // tm, D_FF // tf),
        in_specs=[
            pl.BlockSpec((tm, D_MODEL), row),
            pl.BlockSpec((tm, D_MODEL), row),
            pl.BlockSpec((None, tm, D_PLE), lambda i, f: (layer, i, 0)),
            pl.BlockSpec((D_MODEL, tf), lambda i, f: (0, f), pipeline_mode=weight_mode),
            pl.BlockSpec((D_MODEL, tf), lambda i, f: (0, f), pipeline_mode=weight_mode),
            pl.BlockSpec((tf, D_MODEL), lambda i, f: (f, 0), pipeline_mode=weight_mode),
            pl.BlockSpec((1, D_MODEL), fixed),
            pl.BlockSpec((D_MODEL, D_MODEL), fixed),
            pl.BlockSpec((D_PLE, D_MODEL), fixed),
        ],
        out_specs=pl.BlockSpec((tm, D_MODEL), row),
        out_shape=jax.ShapeDtypeStruct((m, D_MODEL), F32),
        compiler_params=pltpu.CompilerParams(
            dimension_semantics=("arbitrary", "arbitrary"), vmem_limit_bytes=VMEM_LIMIT),
        name="ffn",
    )(x, h, p, wg, wu, wd, npl, wpg, wpl)


def _row_copy(src_ref, src_row, dst_ref, dst_row, sem):
    return pltpu.make_async_copy(src_ref.at[pl.ds(src_row, 1)], dst_ref.at[pl.ds(dst_row, 1)], sem)


def _dispatch_kernel(zstart_ref, zflag_ref, dest_ref, x_ref, xs_ref, zero_ref, zsem, sem, *, tm, tg, tiles):
    i = pl.program_id(0)

    @pl.when(i == 0)
    def _():
        zero_ref[...] = jnp.zeros_like(zero_ref)
        def fill(e):
            return pltpu.make_async_copy(zero_ref, xs_ref.at[pl.ds(pl.multiple_of(zstart_ref[e], tg), tg)], zsem)

        for e in range(N_FILL):
            @pl.when(zflag_ref[e] > 0)
            def _():
                fill(e).start()
        for e in range(N_FILL):
            @pl.when(zflag_ref[e] > 0)
            def _():
                fill(e).wait()

    for k in range(tiles):
        def issue(g, carry, k=k):
            base = pl.multiple_of(g * DMA_ROWS, DMA_ROWS)
            rows = x_ref.at[pl.ds(k * tm + base, DMA_ROWS)]
            for j in range(DMA_ROWS):
                for s in range(TOP_K):
                    _row_copy(rows, j, xs_ref, dest_ref[k, 0, s * tm + base + j], sem).start()
            return carry

        lax.fori_loop(0, tm // DMA_ROWS, issue, 0)

    for s in range(TOP_K):
        pltpu.make_async_copy(x_ref, xs_ref.at[pl.ds(0, tiles * tm)], sem).wait()


def _dispatch(x, dest, zstart, zflag, n_rows, tm, tg):
    m = x.shape[0]
    tiles = DISPATCH_TILES if (m // tm) % DISPATCH_TILES == 0 else 1
    return pl.pallas_call(
        functools.partial(_dispatch_kernel, tm=tm, tg=tg, tiles=tiles),
        grid_spec=pltpu.PrefetchScalarGridSpec(
            num_scalar_prefetch=2,
            grid=(m // (tiles * tm),),
            in_specs=[
                pl.BlockSpec((tiles, 1, TOP_K * tm), lambda i, zs, zf: (i, 0, 0), memory_space=pltpu.SMEM),
                pl.BlockSpec((tiles * tm, D_MODEL), lambda i, zs, zf: (i, 0)),
            ],
            out_specs=pl.BlockSpec(memory_space=pl.ANY),
            scratch_shapes=[
                pltpu.VMEM((tg, D_MODEL), F32),
                pltpu.SemaphoreType.DMA(()),
                pltpu.SemaphoreType.DMA(()),
            ],
        ),
        out_shape=jax.ShapeDtypeStruct((n_rows, D_MODEL), F32),
        compiler_params=pltpu.CompilerParams(
            dimension_semantics=("arbitrary",), vmem_limit_bytes=VMEM_LIMIT),
        name="moe_dispatch",
    )(zstart, zflag, dest, x)


def _moe_ffn_kernel(te_ref, nused_ref, x_ref, wg_ref, wu_ref, wd_ref, o_ref):
    i = pl.program_id(0)
    f = pl.program_id(1)

    @pl.when((i >= nused_ref[0]) & (f == 0))
    def _():
        o_ref[...] = jnp.zeros_like(o_ref)

    @pl.when(i < nused_ref[0])
    def _():
        y = _swiglu(x_ref[...].astype(BF16), wg_ref[0], wu_ref[0], wd_ref[0])

        @pl.when(f == 0)
        def _():
            o_ref[...] = y

        @pl.when(f > 0)
        def _():
            o_ref[...] += y


def _moe_ffn(xs, te, nused, wg, wu, wd, tg, tf):
    n_rows = xs.shape[0]
    nf = D_FF // tf
    tile = lambda i, nu: jnp.minimum(i, nu[0] - 1)
    fstep = lambda i, f, nu: jnp.where(i < nu[0], f, nf - 1)
    return pl.pallas_call(
        _moe_ffn_kernel,
        grid_spec=pltpu.PrefetchScalarGridSpec(
            num_scalar_prefetch=2,
            grid=(n_rows // tg, nf),
            in_specs=[
                pl.BlockSpec((tg, D_MODEL), lambda i, f, te, nu: (tile(i, nu), 0)),
                pl.BlockSpec((1, D_MODEL, tf), lambda i, f, te, nu: (te[tile(i, nu)], 0, fstep(i, f, nu))),
                pl.BlockSpec((1, D_MODEL, tf), lambda i, f, te, nu: (te[tile(i, nu)], 0, fstep(i, f, nu))),
                pl.BlockSpec((1, tf, D_MODEL), lambda i, f, te, nu: (te[tile(i, nu)], fstep(i, f, nu), 0)),
            ],
            out_specs=pl.BlockSpec((tg, D_MODEL), lambda i, f, te, nu: (i, 0)),
        ),
        out_shape=jax.ShapeDtypeStruct((n_rows, D_MODEL), F32),
        compiler_params=pltpu.CompilerParams(
            dimension_semantics=("arbitrary", "arbitrary"), vmem_limit_bytes=VMEM_LIMIT),
        name="moe_ffn",
    )(te, nused, xs, wg, wu, wd)


def _combine_kernel(dest_ref, dest_next_ref, route_ref, h_ref, p_ref, npl_ref, wpg_ref, wpl_ref, ys_ref, o_ref,
                    buf_ref, sem, *, tm):
    i = pl.program_id(0)
    cur = i % 2

    def issue_rows(rows_ref, half, base):
        for s in range(TOP_K):
            rows = buf_ref.at[half, s, pl.ds(base, DMA_ROWS)]
            for j in range(DMA_ROWS):
                _row_copy(ys_ref, rows_ref[0, 0, s * tm + base + j], rows, j, sem.at[half]).start()

    def wait_half(half):
        for s in range(TOP_K):
            pltpu.make_async_copy(ys_ref.at[pl.ds(0, tm)], buf_ref.at[half, s], sem.at[half]).wait()

    @pl.when(i == 0)
    def _():
        def issue(g, carry):
            issue_rows(dest_ref, 0, pl.multiple_of(g * DMA_ROWS, DMA_ROWS))
            return carry

        lax.fori_loop(0, tm // DMA_ROWS, issue, 0)

    wait_half(cur)

    g1 = route_ref[:, 2:3]
    g2 = route_ref[:, 3:4]
    h2 = h_ref[...] + (g1 * buf_ref[cur, 0] + g2 * buf_ref[cur, 1])

    for g in range(tm // DMA_ROWS):
        issue_rows(dest_next_ref, 1 - cur, g * DMA_ROWS)

    o_ref[...] = _ple(h2, p_ref, npl_ref, wpg_ref, wpl_ref)

    @pl.when(i == pl.num_programs(0) - 1)
    def _():
        wait_half(1 - cur)


def _combine(ys, dest, route, h, p, layer, npl, wpg, wpl, tm):
    m = h.shape[0]
    row = lambda i: (i, 0)
    fixed = lambda i: (0, 0)
    return pl.pallas_call(
        functools.partial(_combine_kernel, tm=tm),
        grid=(m // tm,),
        in_specs=[
            pl.BlockSpec((1, 1, TOP_K * tm), lambda i: (i, 0, 0), memory_space=pltpu.SMEM),
            pl.BlockSpec((1, 1, TOP_K * tm), lambda i: (jnp.minimum(i + 1, m // tm - 1), 0, 0),
                         memory_space=pltpu.SMEM),
            pl.BlockSpec((tm, LANES), row),
            pl.BlockSpec((tm, D_MODEL), row),
            pl.BlockSpec((None, tm, D_PLE), lambda i: (layer, i, 0)),
            pl.BlockSpec((1, D_MODEL), fixed),
            pl.BlockSpec((D_MODEL, D_MODEL), fixed),
            pl.BlockSpec((D_PLE, D_MODEL), fixed),
            pl.BlockSpec(memory_space=pl.ANY),
        ],
        out_specs=pl.BlockSpec((tm, D_MODEL), row),
        out_shape=jax.ShapeDtypeStruct((m, D_MODEL), F32),
        scratch_shapes=[
            pltpu.VMEM((2, TOP_K, tm, D_MODEL), F32),
            pltpu.SemaphoreType.DMA((2,)),
        ],
        compiler_params=pltpu.CompilerParams(
            dimension_semantics=("arbitrary",), vmem_limit_bytes=VMEM_LIMIT),
        name="moe_combine",
    )(dest, dest, route, h, p, npl, wpg, wpl, ys)


def _routing(route_t, count, m, tm, tg):
    rows = jnp.transpose(route_t, (1, 0, 2)).reshape(SUBLANES, m)
    counts = count[:N_EXPERTS, 0].astype(jnp.int32)
    padded = (counts + tg - 1) // tg * tg
    ends = jnp.cumsum(padded)
    offs = ends - padded

    def dest_rows(expert, rank):
        expert = expert.astype(jnp.int32)
        start = sum(jnp.where(expert == k, offs[k], 0) for k in range(N_EXPERTS))
        return (start + rank.astype(jnp.int32)).reshape(m // tm, 1, tm)

    dest = jnp.concatenate([dest_rows(rows[s], rows[2 * TOP_K + s]) for s in range(TOP_K)], axis=2)
    n_tiles = TOP_K * m // tg + N_EXPERTS
    nused = ends[-1] // tg
    tile_ids = jnp.minimum(jnp.arange(n_tiles, dtype=jnp.int32), nused - 1)
    te = jnp.sum((tile_ids[:, None] >= (ends // tg)[None, :]).astype(jnp.int32), axis=1)
    te = jnp.minimum(te, N_EXPERTS - 1)
    slack = nused + jnp.arange(N_FILL - N_EXPERTS, dtype=jnp.int32)
    zstart = jnp.concatenate([jnp.maximum(ends - tg, 0), jnp.minimum(slack, n_tiles - 1) * tg])
    zflag = jnp.concatenate([counts > 0, slack < n_tiles]).astype(jnp.int32)
    return (dest, te.astype(jnp.int32), nused.reshape(1).astype(jnp.int32),
            zstart.astype(jnp.int32), zflag, n_tiles * tg)


def _moe(hn, h1, p, layer, route, route_t, count, wg, wu, wd, npl, wpg, wpl, tm, tg, tf):
    dest, te, nused, zstart, zflag, n_rows = _routing(route_t, count, hn.shape[0], tm, tg)
    xs = _dispatch(hn, dest, zstart, zflag, n_rows, tm, tg)
    ys = _moe_ffn(xs, te, nused, wg, wu, wd, tg, tf)
    return _combine(ys, dest, route, h1, p, layer, npl, wpg, wpl, tm)


def _alibi_bias():
    slopes = jnp.exp2(-8.0 * jnp.arange(1, N_HEADS + 1, dtype=F32) / N_HEADS)
    qi = jnp.arange(CHUNK, dtype=jnp.int32)[None, :]
    kj = jnp.arange(BAND, dtype=jnp.int32)[:, None]
    dist = jnp.abs(qi + WINDOW - kj).astype(F32)
    bias = slopes[:, None, None] * dist[None]
    bias = bias.reshape(N_KV_HEADS, GROUP, BAND, CHUNK)
    return jnp.transpose(bias, (0, 2, 1, 3)).reshape(N_KV_HEADS, BAND, GROUP * CHUNK)


def _head_sum_matrix():
    head = jnp.arange(MXU_WIDTH, dtype=jnp.int32) // HEAD_DIM
    return (head[:, None] == head[None, :]).astype(BF16)


def _layer_params(l, norm_mix, w_in, q_gain, k_gain, attn_sinks, conv_w, conv_b, conv_ln_g, conv_ln_b,
                  out_norm_attn, out_norm_conv, w_out, norm_ffn, router_w, router_b, ple_norm,
                  w_ple_gate, w_ple):
    scale = HEAD_DIM ** -0.5
    prm = {
        "norm_mix": norm_mix[l][None, :],
        "w_in": w_in[l].astype(BF16),
        "qk_gain": jnp.concatenate([jnp.tile(q_gain[l] * scale, N_HEADS), jnp.tile(k_gain[l], N_KV_HEADS)])[None, :],
        "sinks": jnp.broadcast_to(attn_sinks[l].reshape(N_KV_HEADS, 1, GROUP, 1),
                                  (N_KV_HEADS, 1, GROUP, CHUNK)).reshape(N_KV_HEADS, 1, GROUP * CHUNK),
        "conv_w": jnp.pad(conv_w[l], ((0, CONV_PAD - CONV_K), (0, 0))),
        "conv_b": conv_b[l][None, :],
        "cln_g": conv_ln_g[l][None, :],
        "cln_b": conv_ln_b[l][None, :],
        "on_attn": out_norm_attn[l][None, :],
        "on_conv": out_norm_conv[l][None, :],
        "w_out": w_out[l].astype(BF16),
        "norm_ffn": norm_ffn[l][None, :],
        "ple_norm": ple_norm[l][None, :],
        "w_ple_gate": w_ple_gate[l].astype(BF16),
        "w_ple": w_ple[l].astype(BF16),
        "router_w": None,
        "router_b": None,
    }
    if l % 2 == 1:
        i = l // 2
        rw = jnp.pad(router_w[i], ((0, 0), (0, LANES - N_EXPERTS)))
        rw_hi = rw.astype(BF16)
        rw_lo = (rw - rw_hi.astype(F32)).astype(BF16)
        prm["router_w"] = jnp.concatenate([rw_hi, rw_lo], axis=1)
        prm["router_b"] = jnp.pad(router_b[i], (0, LANES - N_EXPERTS))[None, :]
    return prm


def _trunk(x, p, cache_k, cache_v, state_conv, layers, ffn_w, bias, hsum, *, tm, tq, nb=1):
    nseq, seq, _ = x.shape
    m = nseq * seq
    has_past = cache_k is not None
    h = x.reshape(m, D_MODEL)
    p2 = p.reshape(p.shape[0], m, D_PLE)
    win_k, win_v, convs = [], [], []
    for l, prm in enumerate(layers):
        q, kv, u = _in_proj(h, prm["norm_mix"], prm["w_in"], prm["qk_gain"], hsum,
                            2 * tm if m % (2 * tm) == 0 else tm)
        if has_past:
            win = cache_k.shape[2]
            kv_prev = jnp.concatenate([cache_k[l].reshape(nseq * win, KV_WIDTH),
                                       cache_v[l].reshape(nseq * win, KV_WIDTH)], axis=1)
            u_prev = jnp.pad(state_conv[l], ((0, 0), (CONV_PAD - (CONV_K - 1), 0), (0, 0))).reshape(
                nseq * CONV_PAD, CONV_CH)
        else:
            kv_prev, u_prev = kv, u
        outs = _mixer(q, kv, kv_prev, u, u_prev, h, dict(prm, bias=bias), nseq=nseq, tq=tq, nb=nb, has_past=has_past)
        h1, hn = outs[0], outs[1]
        wg, wu, wd = ffn_w[l]
        if len(outs) > 2:
            route, route_t, count = outs[2:]
            h = _moe(hn, h1, p2, l, route, route_t, count, wg, wu, wd,
                     prm["ple_norm"], prm["w_ple_gate"], prm["w_ple"], tm, tm // 2, D_FF)
        else:
            h = _ffn(hn, h1, p2, l, wg, wu, wd, prm["ple_norm"], prm["w_ple_gate"], prm["w_ple"], tm // 2, D_FF)
        kv3 = kv.reshape(nseq, seq, 2 * KV_WIDTH)
        u3 = u.reshape(nseq, seq, CONV_CH)
        if has_past:
            kv3 = jnp.concatenate([kv_prev.reshape(nseq, win, 2 * KV_WIDTH), kv3], axis=1)[:, -win:]
            u3 = jnp.concatenate([state_conv[l], u3], axis=1)
        else:
            kv3 = kv3[:, seq - WINDOW:]
        win_k.append(kv3[..., :KV_WIDTH].reshape(nseq, -1, N_KV_HEADS, HEAD_DIM))
        win_v.append(kv3[..., KV_WIDTH:].reshape(nseq, -1, N_KV_HEADS, HEAD_DIM))
        convs.append(u3[:, -(CONV_K - 1):])
    return h.reshape(nseq, seq, D_MODEL), jnp.stack(win_k), jnp.stack(win_v), jnp.stack(convs)


def kernel(x_prompt, x_sample, p_prompt, p_sample, cache_k, cache_v, state_conv, norm_mix, w_in, q_gain, k_gain, attn_sinks, conv_w, conv_b, conv_ln_g, conv_ln_b, out_norm_attn, out_norm_conv, w_out, norm_ffn, ffn_gate, ffn_up, ffn_down, router_w, router_b, moe_gate, moe_up, moe_down, ple_norm, w_ple_gate, w_ple):
    depth = w_in.shape[0]
    layers = [
        _layer_params(l, norm_mix, w_in, q_gain, k_gain, attn_sinks, conv_w, conv_b, conv_ln_g, conv_ln_b,
                      out_norm_attn, out_norm_conv, w_out, norm_ffn, router_w, router_b, ple_norm,
                      w_ple_gate, w_ple)
        for l in range(depth)
    ]
    ffn_w = []
    for l in range(depth):
        i = l // 2
        if l % 2 == 0:
            ffn_w.append((ffn_gate[i].astype(BF16), ffn_up[i].astype(BF16), ffn_down[i].astype(BF16)))
        else:
            ffn_w.append((moe_gate[i].astype(BF16), moe_up[i].astype(BF16), moe_down[i].astype(BF16)))
    bias = _alibi_bias()
    hsum = _head_sum_matrix()
    tm_p = min(TOKEN_TILE, x_prompt.shape[0] * x_prompt.shape[1])
    tq_p = min(TOKEN_TILE, x_prompt.shape[1])
    y_p, wk_p, wv_p, cv_p = _trunk(x_prompt, p_prompt, None, None, None, layers, ffn_w, bias, hsum,
                                   tm=tm_p, tq=tq_p)
    tm_s = min(TOKEN_TILE, x_sample.shape[0] * x_sample.shape[1])
    nb_s = max(d for d in range(1, x_sample.shape[0] + 1)
               if x_sample.shape[0] % d == 0 and d * x_sample.shape[1] <= tm_s)
    y_s, wk_s, wv_s, cv_s = _trunk(x_sample, p_sample, cache_k, cache_v, state_conv, layers, ffn_w, bias, hsum,
                                   tm=tm_s, tq=x_sample.shape[1], nb=nb_s)
    return (y_p, y_s, wk_p, wv_p, cv_p, wk_s, wv_s, cv_s)
```

```python
import functools

import jax
import jax.numpy as jnp
from jax import lax
from jax.experimental import pallas as pl
from jax.experimental.pallas import tpu as pltpu

D_MODEL = 1024
CHUNK = 64
WINDOW = 128
N_HEADS = 8
N_KV_HEADS = 2
HEAD_DIM = 64
GROUP = N_HEADS // N_KV_HEADS
HEAD_PACK = 4
ATTN_WIDTH = N_HEADS * HEAD_DIM
KV_WIDTH = N_KV_HEADS * HEAD_DIM
QK_WIDTH = ATTN_WIDTH + KV_WIDTH
CONV_CH = 512
CONV_K = 31
SUBLANES = 8
CONV_PAD = 32
CONV_ROWS = 64
CONV_ACCS = 4
D_IN = ATTN_WIDTH + 2 * KV_WIDTH + 2 * CONV_CH
BAND = WINDOW + CHUNK
D_FF = 2816
N_EXPERTS = 8
TOP_K = 2
N_FILL = 2 * N_EXPERTS
D_PLE = 256
DMA_ROWS = 8
DISPATCH_TILES = 2
EPS = 1e-6
LANES = 128
MXU_WIDTH = 256
TOKEN_TILE = 512

F32 = jnp.float32
BF16 = jnp.bfloat16

VMEM_LIMIT = 56 * 1024 * 1024

def _rms(x, g):
    return x * lax.rsqrt(jnp.mean(x * x, axis=-1, keepdims=True) + EPS) * g


def _in_proj_kernel(h_ref, nrm_ref, w_ref, gain_ref, hsum_ref, q_ref, kv_ref, u_ref):
    xn = _rms(h_ref[...], nrm_ref[...])
    z = jnp.dot(xn.astype(BF16), w_ref[...], preferred_element_type=F32)
    qk = z[:, :QK_WIDTH]
    sq = (qk * qk).astype(BF16)
    ss = jnp.concatenate(
        [jnp.dot(sq[:, c:min(c + MXU_WIDTH, QK_WIDTH)],
                 hsum_ref[:min(MXU_WIDTH, QK_WIDTH - c), :min(MXU_WIDTH, QK_WIDTH - c)],
                 preferred_element_type=F32)
         for c in range(0, QK_WIDTH, MXU_WIDTH)], axis=1)
    qkn = qk * lax.rsqrt(ss * (1.0 / HEAD_DIM) + EPS) * gain_ref[...]
    q_ref[...] = qkn[:, :ATTN_WIDTH].astype(q_ref.dtype)
    kv_ref[:, :KV_WIDTH] = qkn[:, ATTN_WIDTH:]
    kv_ref[:, KV_WIDTH:] = z[:, QK_WIDTH:QK_WIDTH + KV_WIDTH]
    a = z[:, QK_WIDTH + KV_WIDTH:QK_WIDTH + KV_WIDTH + CONV_CH]
    gl = z[:, QK_WIDTH + KV_WIDTH + CONV_CH:]
    u_ref[...] = a * jax.nn.sigmoid(gl)


def _in_proj(h, nrm, w_in, qk_gain, hsum, tm):
    m = h.shape[0]
    row = lambda i: (i, 0)
    fixed = lambda i: (0, 0)
    return pl.pallas_call(
        _in_proj_kernel,
        grid=(m // tm,),
        in_specs=[
            pl.BlockSpec((tm, D_MODEL), row),
            pl.BlockSpec((1, D_MODEL), fixed),
            pl.BlockSpec((D_MODEL, D_IN), fixed),
            pl.BlockSpec((1, QK_WIDTH), fixed),
            pl.BlockSpec((MXU_WIDTH, MXU_WIDTH), fixed),
        ],
        out_specs=[
            pl.BlockSpec((tm, ATTN_WIDTH), row),
            pl.BlockSpec((tm, 2 * KV_WIDTH), row),
            pl.BlockSpec((tm, CONV_CH), row),
        ],
        out_shape=[
            jax.ShapeDtypeStruct((m, ATTN_WIDTH), BF16),
            jax.ShapeDtypeStruct((m, 2 * KV_WIDTH), F32),
            jax.ShapeDtypeStruct((m, CONV_CH), F32),
        ],
        compiler_params=pltpu.CompilerParams(
            dimension_semantics=("arbitrary",), vmem_limit_bytes=VMEM_LIMIT),
        name="in_proj",
    )(h, nrm, w_in, qk_gain, hsum)


def _mixer_kernel(q_ref, kv_ref, kvp_ref, u_ref, up_ref, h_ref, bias_ref, sink_ref,
                  cw_ref, cb_ref, lng_ref, lnb_ref, ona_ref, onc_ref, wout_ref, nffn_ref,
                  *rest, tq, nb, has_past, with_router):
    if with_router:
        (rw_ref, rb_ref, utri_ref, h1_ref, hn_ref, route_ref, routet_ref, count_ref,
         kvx_ref, ux_ref, oa_ref, cv_ref, s_ref, p_ref, cnt_ref) = rest
    else:
        h1_ref, hn_ref, kvx_ref, ux_ref, oa_ref, cv_ref, s_ref, p_ref = rest
    t = pl.program_id(1)
    nch = tq // CHUNK
    kv_seg = WINDOW + tq
    u_seg = CONV_PAD + tq

    for i in range(nb):
        kvx_ref[i * kv_seg:i * kv_seg + WINDOW, :] = kvp_ref[i * WINDOW:(i + 1) * WINDOW, :].astype(BF16)
        kvx_ref[i * kv_seg + WINDOW:(i + 1) * kv_seg, :] = kv_ref[i * tq:(i + 1) * tq, :].astype(BF16)

    packs = N_HEADS // HEAD_PACK
    units = [(i * tq + c * CHUNK, i * kv_seg + c * CHUNK, c, hp)
             for i in range(nb) for c in range(nch) for hp in range(packs)]

    def unit_operands(hp):
        j = hp * HEAD_PACK // GROUP
        lanes = slice((hp * HEAD_PACK % GROUP) * CHUNK, (hp * HEAD_PACK % GROUP + HEAD_PACK) * CHUNK)
        return j, lanes

    for u, (qrow, krow, c, hp) in enumerate(units):
        j, lanes = unit_operands(hp)
        qc = q_ref[qrow:qrow + CHUNK, :]
        qs = jnp.concatenate([qc[:, h * HEAD_DIM:(h + 1) * HEAD_DIM]
                              for h in range(hp * HEAD_PACK, (hp + 1) * HEAD_PACK)],
                             axis=0)
        kj = kvx_ref[krow:krow + BAND, j * HEAD_DIM:(j + 1) * HEAD_DIM]
        s = lax.dot_general(kj, qs, (((1,), (1,)), ((), ())), preferred_element_type=F32)
        s = s - bias_ref[j, :, lanes]
        if not has_past and c < WINDOW // CHUNK:
            kpos = lax.broadcasted_iota(jnp.int32, (BAND, 1), 0) + (t * tq + c * CHUNK - WINDOW)
            s = jnp.where(kpos >= 0, s, -jnp.inf)
        s_ref[u] = s

    for u, (qrow, krow, c, hp) in enumerate(units):
        j, lanes = unit_operands(hp)
        s = s_ref[u]
        sink = sink_ref[j, :, lanes]
        mx = jnp.maximum(jnp.max(s, axis=0, keepdims=True), sink)
        e = jnp.exp(s - mx)
        den = jnp.sum(e, axis=0, keepdims=True) + jnp.exp(sink - mx)
        p_ref[u] = (e * (1.0 / den)).astype(BF16)

    for u0 in range(0, len(units), packs):
        outs = []
        for u in range(u0, u0 + packs):
            qrow, krow, c, hp = units[u]
            j, _ = unit_operands(hp)
            vj = kvx_ref[krow:krow + BAND, KV_WIDTH + j * HEAD_DIM:KV_WIDTH + (j + 1) * HEAD_DIM]
            o = lax.dot_general(p_ref[u], vj, (((0,), (0,)), ((), ())),
                                preferred_element_type=F32)
            outs.extend(o[g * CHUNK:(g + 1) * CHUNK, :] for g in range(HEAD_PACK))
        oa_ref[qrow:qrow + CHUNK, :] = jnp.concatenate(outs, axis=1)

    for i in range(nb):
        prefix = up_ref[i * CONV_PAD:(i + 1) * CONV_PAD, :]
        ux_ref[0, i * u_seg:i * u_seg + CONV_PAD, :] = prefix if has_past else jnp.where(t > 0, prefix, 0.0)
        ux_ref[0, i * u_seg + CONV_PAD:(i + 1) * u_seg, :] = u_ref[i * tq:(i + 1) * tq, :]
    shifted_rows = nb * u_seg - SUBLANES
    for r in range(1, SUBLANES):
        ux_ref[r, :shifted_rows, :] = ux_ref[0, r:r + shifted_rows, :]
    lead = CONV_PAD - (CONV_K - 1)

    for cg in range(CONV_CH // LANES):
        lanes = slice(cg * LANES, (cg + 1) * LANES)
        taps = [cw_ref[k:k + 1, lanes] for k in range(CONV_K)]

        def conv_rows(i, carry, lanes=lanes, taps=taps):
            out0 = pl.multiple_of(i * CONV_ROWS, CONV_ROWS)
            r0 = out0 if nb == 1 else pl.multiple_of(i * u_seg, SUBLANES)
            accs = [None] * CONV_ACCS
            for r in range(SUBLANES):
                steps = [(k, (lead + k) // SUBLANES) for k in range(CONV_K) if (lead + k) % SUBLANES == r]
                slab = ux_ref[r, pl.ds(r0, CONV_ROWS + steps[-1][1] * SUBLANES), lanes]
                for k, a in steps:
                    term = slab[a * SUBLANES:a * SUBLANES + CONV_ROWS, :] * taps[k]
                    accs[k % CONV_ACCS] = term if accs[k % CONV_ACCS] is None else accs[k % CONV_ACCS] + term
            cv_ref[pl.ds(out0, CONV_ROWS), lanes] = functools.reduce(lambda x, y: x + y, accs) + cb_ref[:, lanes]
            return carry

        lax.fori_loop(0, nb * tq // CONV_ROWS, conv_rows, 0)

    cv = cv_ref[...]
    mu = jnp.mean(cv, axis=-1, keepdims=True)
    xc = cv - mu
    ln = xc * lax.rsqrt(jnp.mean(xc * xc, axis=-1, keepdims=True) + EPS) * lng_ref[...] + lnb_ref[...]
    oc = ln * jax.nn.sigmoid(ln)
    cn = _rms(oc, onc_ref[...]).astype(BF16)

    an = _rms(oa_ref[...], ona_ref[...]).astype(BF16)
    mixed = jnp.dot(jnp.concatenate([an, cn], axis=1), wout_ref[...], preferred_element_type=F32)
    h1 = h_ref[...] + mixed
    h1_ref[...] = h1
    hn = _rms(h1, nffn_ref[...])
    hn_ref[...] = hn.reshape(hn_ref.shape) if with_router else hn.astype(hn_ref.dtype)

    if with_router:
        hi = hn.astype(BF16)
        lo = (hn - hi.astype(F32)).astype(BF16)
        part = jnp.dot(hi, rw_ref[...], preferred_element_type=F32)
        logits = (part[:, :LANES] + part[:, LANES:]
                  + jnp.dot(lo, rw_ref[:, :LANES], preferred_element_type=F32) + rb_ref[...])
        lt = logits.T[:N_EXPERTS, :]
        row = lax.broadcasted_iota(jnp.int32, lt.shape, 0)
        m1 = jnp.max(lt, axis=0, keepdims=True)
        i1 = jnp.min(jnp.where(lt == m1, row, N_EXPERTS), axis=0, keepdims=True)
        rest_l = jnp.where(row == i1, -jnp.inf, lt)
        m2 = jnp.max(rest_l, axis=0, keepdims=True)
        i2 = jnp.min(jnp.where(rest_l == m2, row, N_EXPERTS), axis=0, keepdims=True)
        e2 = jnp.exp(m2 - m1)
        den = 1.0 + e2

        @pl.when((pl.program_id(0) == 0) & (t == 0))
        def _():
            cnt_ref[...] = jnp.zeros_like(cnt_ref)

        sel = ((row == i1) | (row == i2)).astype(F32)
        ahead = jnp.dot(sel.astype(BF16), utri_ref[...], preferred_element_type=F32) + cnt_ref[:, 0:1]
        r1 = jnp.sum(jnp.where(row == i1, ahead, 0.0), axis=0, keepdims=True)
        r2 = jnp.sum(jnp.where(row == i2, ahead, 0.0), axis=0, keepdims=True)
        cnt_ref[...] += jnp.sum(sel, axis=1, keepdims=True)
        count_ref[...] = cnt_ref[...]

        route_t = jnp.concatenate(
            [i1.astype(F32), i2.astype(F32), 1.0 / den, e2 / den, r1, r2,
             jnp.zeros((SUBLANES - 3 * TOP_K, nb * tq), F32)], axis=0)
        routet_ref[0] = route_t
        route_ref[...] = jnp.concatenate([route_t, jnp.zeros((LANES - SUBLANES, nb * tq), F32)], axis=0).T


def _mixer(q, kv, kv_prev, u, u_prev, h, prm, *, nseq, tq, nb, has_past):
    m = h.shape[0]
    seq = m // nseq
    nt = seq // tq
    assert nb == 1 or (has_past and nt == 1 and tq == CONV_ROWS and nseq % nb == 0)
    tr = nb * tq
    with_router = prm["router_w"] is not None
    cur = lambda b, t: (b * nt + t, 0)
    fixed2 = lambda b, t: (0, 0)
    fixed3 = lambda b, t: (0, 0, 0)
    if has_past:
        kvp_map = lambda b, t: (b, 0)
        up_map = lambda b, t: (b, 0)
    else:
        kvp_map = lambda b, t: (jnp.maximum(b * (seq // WINDOW) + t * (tq // WINDOW) - 1, 0), 0)
        up_map = lambda b, t: (jnp.maximum(b * (seq // CONV_PAD) + t * (tq // CONV_PAD) - 1, 0), 0)
    in_specs = [
        pl.BlockSpec((tr, ATTN_WIDTH), cur),
        pl.BlockSpec((tr, 2 * KV_WIDTH), cur),
        pl.BlockSpec((nb * WINDOW, 2 * KV_WIDTH), kvp_map),
        pl.BlockSpec((tr, CONV_CH), cur),
        pl.BlockSpec((nb * CONV_PAD, CONV_CH), up_map),
        pl.BlockSpec((tr, D_MODEL), cur),
        pl.BlockSpec((N_KV_HEADS, BAND, GROUP * CHUNK), fixed3),
        pl.BlockSpec((N_KV_HEADS, 1, GROUP * CHUNK), fixed3),
        pl.BlockSpec((CONV_PAD, CONV_CH), fixed2),
        pl.BlockSpec((1, CONV_CH), fixed2),
        pl.BlockSpec((1, CONV_CH), fixed2),
        pl.BlockSpec((1, CONV_CH), fixed2),
        pl.BlockSpec((1, ATTN_WIDTH), fixed2),
        pl.BlockSpec((1, CONV_CH), fixed2),
        pl.BlockSpec((D_MODEL, D_MODEL), fixed2),
        pl.BlockSpec((1, D_MODEL), fixed2),
    ]
    args = [q, kv, kv_prev, u, u_prev, h, prm["bias"], prm["sinks"], prm["conv_w"], prm["conv_b"],
            prm["cln_g"], prm["cln_b"], prm["on_attn"], prm["on_conv"], prm["w_out"], prm["norm_ffn"]]
    out_specs = [pl.BlockSpec((tr, D_MODEL), cur),
                 pl.BlockSpec((tr, SUBLANES, LANES), lambda b, t: (b * nt + t, 0, 0)) if with_router
                 else pl.BlockSpec((tr, D_MODEL), cur)]
    out_shape = [jax.ShapeDtypeStruct((m, D_MODEL), F32),
                 jax.ShapeDtypeStruct((m, SUBLANES, LANES), F32) if with_router
                 else jax.ShapeDtypeStruct((m, D_MODEL), BF16)]
    n_units = tr // CHUNK * (N_HEADS // HEAD_PACK)
    scratch = [
        pltpu.VMEM((nb * (WINDOW + tq), 2 * KV_WIDTH), BF16),
        pltpu.VMEM((SUBLANES, nb * (CONV_PAD + tq), CONV_CH), F32),
        pltpu.VMEM((tr, ATTN_WIDTH), F32),
        pltpu.VMEM((tr, CONV_CH), F32),
        pltpu.VMEM((n_units, BAND, HEAD_PACK * CHUNK), F32),
        pltpu.VMEM((n_units, BAND, HEAD_PACK * CHUNK), BF16),
    ]
    if with_router:
        utri = (jnp.arange(tr, dtype=jnp.int32)[:, None] < jnp.arange(tr, dtype=jnp.int32)[None, :]).astype(BF16)
        in_specs += [pl.BlockSpec((D_MODEL, 2 * LANES), fixed2), pl.BlockSpec((1, LANES), fixed2),
                     pl.BlockSpec((tr, tr), fixed2)]
        args += [prm["router_w"], prm["router_b"], utri]
        out_specs += [pl.BlockSpec((tr, LANES), cur),
                      pl.BlockSpec((1, SUBLANES, tr), lambda b, t: (b * nt + t, 0, 0)),
                      pl.BlockSpec((SUBLANES, LANES), fixed2)]
        out_shape += [jax.ShapeDtypeStruct((m, LANES), F32),
                      jax.ShapeDtypeStruct((m // tr, SUBLANES, tr), F32),
                      jax.ShapeDtypeStruct((SUBLANES, LANES), F32)]
        scratch.append(pltpu.VMEM((SUBLANES, LANES), F32))
    return pl.pallas_call(
        functools.partial(_mixer_kernel, tq=tq, nb=nb, has_past=has_past, with_router=with_router),
        grid=(nseq // nb, nt),
        in_specs=in_specs,
        out_specs=out_specs,
        out_shape=out_shape,
        scratch_shapes=scratch,
        compiler_params=pltpu.CompilerParams(
            dimension_semantics=("arbitrary", "arbitrary"), vmem_limit_bytes=VMEM_LIMIT),
        name="mixer_past" if has_past else "mixer",
    )(*args)


def _ple(h2, p_ref, npl_ref, wpg_ref, wpl_ref):
    r = _rms(h2, npl_ref[...]).astype(BF16)
    gate = jax.nn.sigmoid(jnp.dot(r, wpg_ref[...], preferred_element_type=F32))
    pe = jnp.dot(p_ref[...].astype(BF16), wpl_ref[...], preferred_element_type=F32)
    return h2 + gate * pe


def _swiglu(x, wg, wu, wd):
    g = jnp.dot(x, wg, preferred_element_type=F32)
    u = jnp.dot(x, wu, preferred_element_type=F32)
    a = (g * jax.nn.sigmoid(g) * u).astype(BF16)
    return jnp.dot(a, wd, preferred_element_type=F32)


def _ffn_kernel(x_ref, h_ref, p_ref, wg_ref, wu_ref, wd_ref, npl_ref, wpg_ref, wpl_ref, o_ref):
    f = pl.program_id(1)

    @pl.when(f == 0)
    def _():
        o_ref[...] = h_ref[...]

    o_ref[...] += _swiglu(x_ref[...], wg_ref[...], wu_ref[...], wd_ref[...])

    @pl.when(f == pl.num_programs(1) - 1)
    def _():
        o_ref[...] = _ple(o_ref[...], p_ref, npl_ref, wpg_ref, wpl_ref)


def _ffn(x, h, p, layer, wg, wu, wd, npl, wpg, wpl, tm, tf):
    m = x.shape[0]
    row = lambda i, f: (i, 0)
    fixed = lambda i, f: (0, 0)
    weight_mode = pl.Buffered(1) if tf == D_FF else None
    return pl.pallas_call(
        _ffn_kernel,
        grid=(m // tm, D_FF // tf),
        in_specs=[
            pl.BlockSpec((tm, D_MODEL), row),
            pl.BlockSpec((tm, D_MODEL), row),
            pl.BlockSpec((None, tm, D_PLE), lambda i, f: (layer, i, 0)),
            pl.BlockSpec((D_MODEL, tf), lambda i, f: (0, f), pipeline_mode=weight_mode),
            pl.BlockSpec((D_MODEL, tf), lambda i, f: (0, f), pipeline_mode=weight_mode),
            pl.BlockSpec((tf, D_MODEL), lambda i, f: (f, 0), pipeline_mode=weight_mode),
            pl.BlockSpec((1, D_MODEL), fixed),
            pl.BlockSpec((D_MODEL, D_MODEL), fixed),
            pl.BlockSpec((D_PLE, D_MODEL), fixed),
        ],
        out_specs=pl.BlockSpec((tm, D_MODEL), row),
        out_shape=jax.ShapeDtypeStruct((m, D_MODEL), F32),
        compiler_params=pltpu.CompilerParams(
            dimension_semantics=("arbitrary", "arbitrary"), vmem_limit_bytes=VMEM_LIMIT),
        name="ffn",
    )(x, h, p, wg, wu, wd, npl, wpg, wpl)


def _row_copy(src_ref, src_row, dst_ref, dst_row, sem):
    return pltpu.make_async_copy(src_ref.at[pl.ds(src_row, 1)], dst_ref.at[pl.ds(dst_row, 1)], sem)


def _dispatch_kernel(zstart_ref, zflag_ref, dest_ref, x_ref, xs_ref, zero_ref, zsem, sem, *, tm, tg, tiles):
    i = pl.program_id(0)

    @pl.when(i == 0)
    def _():
        zero_ref[...] = jnp.zeros_like(zero_ref)
        def fill(e):
            return pltpu.make_async_copy(zero_ref, xs_ref.at[pl.ds(zstart_ref[e], tg)], zsem)

        for e in range(N_FILL):
            @pl.when(zflag_ref[e] > 0)
            def _():
                fill(e).start()
        for e in range(N_FILL):
            @pl.when(zflag_ref[e] > 0)
            def _():
                fill(e).wait()

    for k in range(tiles):
        def issue(g, carry, k=k):
            base = pl.multiple_of(g * DMA_ROWS, DMA_ROWS)
            rows = x_ref.at[pl.ds(k * tm + base, DMA_ROWS)]
            for j in range(DMA_ROWS):
                for s in range(TOP_K):
                    _row_copy(rows, j, xs_ref, dest_ref[k, 0, s * tm + base + j], sem).start()
            return carry

        lax.fori_loop(0, tm // DMA_ROWS, issue, 0)

    for s in range(TOP_K):
        pltpu.make_async_copy(x_ref, xs_ref.at[pl.ds(0, tiles * tm)], sem).wait()


def _dispatch(x, dest, zstart, zflag, n_rows, tm, tg):
    m = x.shape[0]
    tiles = DISPATCH_TILES if (m // tm) % DISPATCH_TILES == 0 else 1
    return pl.pallas_call(
        functools.partial(_dispatch_kernel, tm=tm, tg=tg, tiles=tiles),
        grid_spec=pltpu.PrefetchScalarGridSpec(
            num_scalar_prefetch=2,
            grid=(m // (tiles * tm),),
            in_specs=[
                pl.BlockSpec((tiles, 1, TOP_K * tm), lambda i, zs, zf: (i, 0, 0), memory_space=pltpu.SMEM),
                pl.BlockSpec((tiles * tm, SUBLANES, LANES), lambda i, zs, zf: (i, 0, 0)),
            ],
            out_specs=pl.BlockSpec(memory_space=pl.ANY),
            scratch_shapes=[
                pltpu.VMEM((tg, SUBLANES, LANES), F32),
                pltpu.SemaphoreType.DMA(()),
                pltpu.SemaphoreType.DMA(()),
            ],
        ),
        out_shape=jax.ShapeDtypeStruct((n_rows, SUBLANES, LANES), F32),
        compiler_params=pltpu.CompilerParams(
            dimension_semantics=("arbitrary",), vmem_limit_bytes=VMEM_LIMIT),
        name="moe_dispatch",
    )(zstart, zflag, dest, x)


def _moe_ffn_kernel(te_ref, nused_ref, x_ref, wg_ref, wu_ref, wd_ref, o_ref):
    i = pl.program_id(0)
    f = pl.program_id(1)

    @pl.when((i >= nused_ref[0]) & (f == 0))
    def _():
        o_ref[...] = jnp.zeros_like(o_ref)

    @pl.when(i < nused_ref[0])
    def _():
        y = _swiglu(x_ref[...].reshape(x_ref.shape[0], D_MODEL).astype(BF16), wg_ref[0], wu_ref[0], wd_ref[0])
        y = y.reshape(o_ref.shape)

        @pl.when(f == 0)
        def _():
            o_ref[...] = y

        @pl.when(f > 0)
        def _():
            o_ref[...] += y


def _moe_ffn(xs, te, nused, wg, wu, wd, tg, tf):
    n_rows = xs.shape[0]
    nf = D_FF // tf
    tile = lambda i, nu: jnp.minimum(i, nu[0] - 1)
    fstep = lambda i, f, nu: jnp.where(i < nu[0], f, nf - 1)
    return pl.pallas_call(
        _moe_ffn_kernel,
        grid_spec=pltpu.PrefetchScalarGridSpec(
            num_scalar_prefetch=2,
            grid=(n_rows // tg, nf),
            in_specs=[
                pl.BlockSpec((tg, SUBLANES, LANES), lambda i, f, te, nu: (tile(i, nu), 0, 0)),
                pl.BlockSpec((1, D_MODEL, tf), lambda i, f, te, nu: (te[tile(i, nu)], 0, fstep(i, f, nu))),
                pl.BlockSpec((1, D_MODEL, tf), lambda i, f, te, nu: (te[tile(i, nu)], 0, fstep(i, f, nu))),
                pl.BlockSpec((1, tf, D_MODEL), lambda i, f, te, nu: (te[tile(i, nu)], fstep(i, f, nu), 0)),
            ],
            out_specs=pl.BlockSpec((tg, SUBLANES, LANES), lambda i, f, te, nu: (i, 0, 0)),
        ),
        out_shape=jax.ShapeDtypeStruct((n_rows, SUBLANES, LANES), F32),
        compiler_params=pltpu.CompilerParams(
            dimension_semantics=("arbitrary", "arbitrary"), vmem_limit_bytes=VMEM_LIMIT),
        name="moe_ffn",
    )(te, nused, xs, wg, wu, wd)


def _combine_kernel(dest_ref, dest_next_ref, route_ref, h_ref, p_ref, npl_ref, wpg_ref, wpl_ref, ys_ref, o_ref,
                    buf_ref, sem, *, tm):
    i = pl.program_id(0)
    cur = i % 2

    def issue_rows(rows_ref, half, base):
        for s in range(TOP_K):
            rows = buf_ref.at[half, s, pl.ds(base, DMA_ROWS)]
            for j in range(DMA_ROWS):
                _row_copy(ys_ref, rows_ref[0, 0, s * tm + base + j], rows, j, sem.at[half]).start()

    def wait_half(half):
        for s in range(TOP_K):
            pltpu.make_async_copy(ys_ref.at[pl.ds(0, tm)], buf_ref.at[half, s], sem.at[half]).wait()

    @pl.when(i == 0)
    def _():
        def issue(g, carry):
            issue_rows(dest_ref, 0, pl.multiple_of(g * DMA_ROWS, DMA_ROWS))
            return carry

        lax.fori_loop(0, tm // DMA_ROWS, issue, 0)

    wait_half(cur)

    g1 = route_ref[:, 2:3]
    g2 = route_ref[:, 3:4]
    h2 = h_ref[...] + (g1 * buf_ref[cur, 0].reshape(tm, D_MODEL) + g2 * buf_ref[cur, 1].reshape(tm, D_MODEL))

    for g in range(tm // DMA_ROWS):
        issue_rows(dest_next_ref, 1 - cur, g * DMA_ROWS)

    o_ref[...] = _ple(h2, p_ref, npl_ref, wpg_ref, wpl_ref)

    @pl.when(i == pl.num_programs(0) - 1)
    def _():
        wait_half(1 - cur)


def _combine(ys, dest, route, h, p, layer, npl, wpg, wpl, tm):
    m = h.shape[0]
    row = lambda i: (i, 0)
    fixed = lambda i: (0, 0)
    return pl.pallas_call(
        functools.partial(_combine_kernel, tm=tm),
        grid=(m // tm,),
        in_specs=[
            pl.BlockSpec((1, 1, TOP_K * tm), lambda i: (i, 0, 0), memory_space=pltpu.SMEM),
            pl.BlockSpec((1, 1, TOP_K * tm), lambda i: (jnp.minimum(i + 1, m // tm - 1), 0, 0),
                         memory_space=pltpu.SMEM),
            pl.BlockSpec((tm, LANES), row),
            pl.BlockSpec((tm, D_MODEL), row),
            pl.BlockSpec((None, tm, D_PLE), lambda i: (layer, i, 0)),
            pl.BlockSpec((1, D_MODEL), fixed),
            pl.BlockSpec((D_MODEL, D_MODEL), fixed),
            pl.BlockSpec((D_PLE, D_MODEL), fixed),
            pl.BlockSpec(memory_space=pl.ANY),
        ],
        out_specs=pl.BlockSpec((tm, D_MODEL), row),
        out_shape=jax.ShapeDtypeStruct((m, D_MODEL), F32),
        scratch_shapes=[
            pltpu.VMEM((2, TOP_K, tm, SUBLANES, LANES), F32),
            pltpu.SemaphoreType.DMA((2,)),
        ],
        compiler_params=pltpu.CompilerParams(
            dimension_semantics=("arbitrary",), vmem_limit_bytes=VMEM_LIMIT),
        name="moe_combine",
    )(dest, dest, route, h, p, npl, wpg, wpl, ys)


def _routing(route_t, count, m, tm, tg):
    rows = jnp.transpose(route_t, (1, 0, 2)).reshape(SUBLANES, m)
    counts = count[:N_EXPERTS, 0].astype(jnp.int32)
    padded = (counts + tg - 1) // tg * tg
    ends = jnp.cumsum(padded)
    offs = ends - padded

    def dest_rows(expert, rank):
        expert = expert.astype(jnp.int32)
        start = sum(jnp.where(expert == k, offs[k], 0) for k in range(N_EXPERTS))
        return (start + rank.astype(jnp.int32)).reshape(m // tm, 1, tm)

    dest = jnp.concatenate([dest_rows(rows[s], rows[2 * TOP_K + s]) for s in range(TOP_K)], axis=2)
    n_tiles = TOP_K * m // tg + N_EXPERTS
    nused = ends[-1] // tg
    tile_ids = jnp.minimum(jnp.arange(n_tiles, dtype=jnp.int32), nused - 1)
    te = jnp.sum((tile_ids[:, None] >= (ends // tg)[None, :]).astype(jnp.int32), axis=1)
    te = jnp.minimum(te, N_EXPERTS - 1)
    slack = nused + jnp.arange(N_FILL - N_EXPERTS, dtype=jnp.int32)
    zstart = jnp.concatenate([jnp.maximum(ends - tg, 0), jnp.minimum(slack, n_tiles - 1) * tg])
    zflag = jnp.concatenate([counts > 0, slack < n_tiles]).astype(jnp.int32)
    return (dest, te.astype(jnp.int32), nused.reshape(1).astype(jnp.int32),
            zstart.astype(jnp.int32), zflag, n_tiles * tg)


def _moe(hn, h1, p, layer, route, route_t, count, wg, wu, wd, npl, wpg, wpl, tm, tg, tf):
    dest, te, nused, zstart, zflag, n_rows = _routing(route_t, count, hn.shape[0], tm, tg)
    xs = _dispatch(hn, dest, zstart, zflag, n_rows, tm, tg)
    ys = _moe_ffn(xs, te, nused, wg, wu, wd, tg, tf)
    return _combine(ys, dest, route, h1, p, layer, npl, wpg, wpl, tm)


def _alibi_bias():
    slopes = jnp.exp2(-8.0 * jnp.arange(1, N_HEADS + 1, dtype=F32) / N_HEADS)
    qi = jnp.arange(CHUNK, dtype=jnp.int32)[None, :]
    kj = jnp.arange(BAND, dtype=jnp.int32)[:, None]
    dist = jnp.abs(qi + WINDOW - kj).astype(F32)
    bias = slopes[:, None, None] * dist[None]
    bias = bias.reshape(N_KV_HEADS, GROUP, BAND, CHUNK)
    return jnp.transpose(bias, (0, 2, 1, 3)).reshape(N_KV_HEADS, BAND, GROUP * CHUNK)


def _head_sum_matrix():
    head = jnp.arange(MXU_WIDTH, dtype=jnp.int32) // HEAD_DIM
    return (head[:, None] == head[None, :]).astype(BF16)


def _layer_params(l, norm_mix, w_in, q_gain, k_gain, attn_sinks, conv_w, conv_b, conv_ln_g, conv_ln_b,
                  out_norm_attn, out_norm_conv, w_out, norm_ffn, router_w, router_b, ple_norm,
                  w_ple_gate, w_ple):
    scale = HEAD_DIM ** -0.5
    prm = {
        "norm_mix": norm_mix[l][None, :],
        "w_in": w_in[l].astype(BF16),
        "qk_gain": jnp.concatenate([jnp.tile(q_gain[l] * scale, N_HEADS), jnp.tile(k_gain[l], N_KV_HEADS)])[None, :],
        "sinks": jnp.broadcast_to(attn_sinks[l].reshape(N_KV_HEADS, 1, GROUP, 1),
                                  (N_KV_HEADS, 1, GROUP, CHUNK)).reshape(N_KV_HEADS, 1, GROUP * CHUNK),
        "conv_w": jnp.pad(conv_w[l], ((0, CONV_PAD - CONV_K), (0, 0))),
        "conv_b": conv_b[l][None, :],
        "cln_g": conv_ln_g[l][None, :],
        "cln_b": conv_ln_b[l][None, :],
        "on_attn": out_norm_attn[l][None, :],
        "on_conv": out_norm_conv[l][None, :],
        "w_out": w_out[l].astype(BF16),
        "norm_ffn": norm_ffn[l][None, :],
        "ple_norm": ple_norm[l][None, :],
        "w_ple_gate": w_ple_gate[l].astype(BF16),
        "w_ple": w_ple[l].astype(BF16),
        "router_w": None,
        "router_b": None,
    }
    if l % 2 == 1:
        i = l // 2
        rw = jnp.pad(router_w[i], ((0, 0), (0, LANES - N_EXPERTS)))
        rw_hi = rw.astype(BF16)
        rw_lo = (rw - rw_hi.astype(F32)).astype(BF16)
        prm["router_w"] = jnp.concatenate([rw_hi, rw_lo], axis=1)
        prm["router_b"] = jnp.pad(router_b[i], (0, LANES - N_EXPERTS))[None, :]
    return prm


def _trunk(x, p, cache_k, cache_v, state_conv, layers, ffn_w, bias, hsum, *, tm, tq, nb=1):
    nseq, seq, _ = x.shape
    m = nseq * seq
    has_past = cache_k is not None
    h = x.reshape(m, D_MODEL)
    p2 = p.reshape(p.shape[0], m, D_PLE)
    win_k, win_v, convs = [], [], []
    for l, prm in enumerate(layers):
        q, kv, u = _in_proj(h, prm["norm_mix"], prm["w_in"], prm["qk_gain"], hsum,
                            2 * tm if m % (2 * tm) == 0 else tm)
        if has_past:
            win = cache_k.shape[2]
            kv_prev = jnp.concatenate([cache_k[l].reshape(nseq * win, KV_WIDTH),
                                       cache_v[l].reshape(nseq * win, KV_WIDTH)], axis=1)
            u_prev = jnp.pad(state_conv[l], ((0, 0), (CONV_PAD - (CONV_K - 1), 0), (0, 0))).reshape(
                nseq * CONV_PAD, CONV_CH)
        else:
            kv_prev, u_prev = kv, u
        outs = _mixer(q, kv, kv_prev, u, u_prev, h, dict(prm, bias=bias), nseq=nseq, tq=tq, nb=nb, has_past=has_past)
        h1, hn = outs[0], outs[1]
        wg, wu, wd = ffn_w[l]
        if len(outs) > 2:
            route, route_t, count = outs[2:]
            h = _moe(hn, h1, p2, l, route, route_t, count, wg, wu, wd,
                     prm["ple_norm"], prm["w_ple_gate"], prm["w_ple"], tm, tm // 2, D_FF)
        else:
            h = _ffn(hn, h1, p2, l, wg, wu, wd, prm["ple_norm"], prm["w_ple_gate"], prm["w_ple"], tm // 2, D_FF)
        kv3 = kv.reshape(nseq, seq, 2 * KV_WIDTH)
        u3 = u.reshape(nseq, seq, CONV_CH)
        if has_past:
            kv3 = jnp.concatenate([kv_prev.reshape(nseq, win, 2 * KV_WIDTH), kv3], axis=1)[:, -win:]
            u3 = jnp.concatenate([state_conv[l], u3], axis=1)
        else:
            kv3 = kv3[:, seq - WINDOW:]
        win_k.append(kv3[..., :KV_WIDTH].reshape(nseq, -1, N_KV_HEADS, HEAD_DIM))
        win_v.append(kv3[..., KV_WIDTH:].reshape(nseq, -1, N_KV_HEADS, HEAD_DIM))
        convs.append(u3[:, -(CONV_K - 1):])
    return h.reshape(nseq, seq, D_MODEL), jnp.stack(win_k), jnp.stack(win_v), jnp.stack(convs)


def kernel(x_prompt, x_sample, p_prompt, p_sample, cache_k, cache_v, state_conv, norm_mix, w_in, q_gain, k_gain, attn_sinks, conv_w, conv_b, conv_ln_g, conv_ln_b, out_norm_attn, out_norm_conv, w_out, norm_ffn, ffn_gate, ffn_up, ffn_down, router_w, router_b, moe_gate, moe_up, moe_down, ple_norm, w_ple_gate, w_ple):
    depth = w_in.shape[0]
    layers = [
        _layer_params(l, norm_mix, w_in, q_gain, k_gain, attn_sinks, conv_w, conv_b, conv_ln_g, conv_ln_b,
                      out_norm_attn, out_norm_conv, w_out, norm_ffn, router_w, router_b, ple_norm,
                      w_ple_gate, w_ple)
        for l in range(depth)
    ]
    ffn_w = []
    for l in range(depth):
        i = l // 2
        if l % 2 == 0:
            ffn_w.append((ffn_gate[i].astype(BF16), ffn_up[i].astype(BF16), ffn_down[i].astype(BF16)))
        else:
            ffn_w.append((moe_gate[i].astype(BF16), moe_up[i].astype(BF16), moe_down[i].astype(BF16)))
    bias = _alibi_bias()
    hsum = _head_sum_matrix()
    tm_p = min(TOKEN_TILE, x_prompt.shape[0] * x_prompt.shape[1])
    tq_p = min(TOKEN_TILE, x_prompt.shape[1])
    y_p, wk_p, wv_p, cv_p = _trunk(x_prompt, p_prompt, None, None, None, layers, ffn_w, bias, hsum,
                                   tm=tm_p, tq=tq_p)
    tm_s = min(TOKEN_TILE, x_sample.shape[0] * x_sample.shape[1])
    nb_s = max(d for d in range(1, x_sample.shape[0] + 1)
               if x_sample.shape[0] % d == 0 and d * x_sample.shape[1] <= tm_s)
    y_s, wk_s, wv_s, cv_s = _trunk(x_sample, p_sample, cache_k, cache_v, state_conv, layers, ffn_w, bias, hsum,
                                   tm=tm_s, tq=x_sample.shape[1], nb=nb_s)
    return (y_p, y_s, wk_p, wv_p, cv_p, wk_s, wv_s, cv_s)
```
